```python
import math
import jax, jax.numpy as jnp
from jax import lax
import numpy as np

D_MODEL = 1024
BATCH = 4
SEQ = 8192
DEPTH = 2

MEM_LEN = 256
BLK = 128
EPS = 1e-6
A_HEADS = 8
A_HEAD_DIM = 64
A_PATTERNS = ((128, 1), (512, 4), (2048, 16))
B_CHANNELS = 512
B_CONV_WIDTH = 31
C_HEADS = 4
C_HEAD_DIM = 64
D_HEADS = 8
D_NOPE_DIM = 64
D_ROPE_DIM = 32
D_V_DIM = 64
D_Q_RANK = 384
D_KV_RANK = 256
ROPE_THETA = 10000.0
X_HEADS = 4
X_HEAD_DIM = 128
D_FF = 4 * D_MODEL

A_WIDTH = A_HEADS * A_HEAD_DIM
AB_IN = 3 * A_WIDTH + 2 * B_CHANNELS
AB_OUT = A_WIDTH + B_CHANNELS
C_WIDTH = C_HEADS * 2 * C_HEAD_DIM
CD_IN = 3 * C_WIDTH + D_Q_RANK + D_KV_RANK + D_ROPE_DIM
CD_OUT = C_WIDTH + D_HEADS * D_V_DIM
N_EVEN = (DEPTH + 1) // 2
N_ODD = DEPTH // 2

kernel_name = 'hybrid_dilated_conformer_diff_mla'


def rms_norm(x, g):
    x32 = x.astype(jnp.float32)
    y = x32 * lax.rsqrt(jnp.mean(x32 * x32, axis=-1, keepdims=True) + EPS)
    return (y * g.astype(jnp.float32)).astype(x.dtype)


def layer_norm(x, g, b):
    x32 = x.astype(jnp.float32)
    mu = jnp.mean(x32, axis=-1, keepdims=True)
    var = jnp.mean(jnp.square(x32 - mu), axis=-1, keepdims=True)
    y = (x32 - mu) * lax.rsqrt(var + EPS)
    return (y * g.astype(jnp.float32) + b.astype(jnp.float32)).astype(x.dtype)


def rope(x, positions):
    half = x.shape[-1] // 2
    inv_freq = ROPE_THETA ** (-jnp.arange(half, dtype=jnp.float32) / half)
    ang = positions.astype(jnp.float32)[..., None] * inv_freq
    cos, sin = jnp.cos(ang)[:, :, None, :], jnp.sin(ang)[:, :, None, :]
    x1, x2 = x[..., :half].astype(jnp.float32), x[..., half:].astype(jnp.float32)
    return jnp.concatenate([x1 * cos - x2 * sin, x2 * cos + x1 * sin], axis=-1).astype(x.dtype)


def dilated_window_attention(q, k, v, dilation, steps):
    B, S, H, Dh = q.shape
    assert steps <= BLK
    span = dilation * BLK
    s_pad = -(-S // span) * span
    n_blk = s_pad // span

    def to_sub(t):
        t = jnp.pad(t, ((0, 0), (0, s_pad - S), (0, 0), (0, 0)))
        return t.reshape(B, n_blk, BLK, dilation, H, Dh).transpose(0, 4, 3, 1, 2, 5)

    def with_prev(t):
        prev = jnp.pad(t, ((0, 0), (0, 0), (0, 0), (1, 0), (0, 0), (0, 0)))[:, :, :, :-1]
        return jnp.concatenate([prev, t], axis=4)

    qs = to_sub(q)
    kb, vb = with_prev(to_sub(k)), with_prev(to_sub(v))
    s = jnp.einsum('bhrnqd,bhrnkd->bhrnqk', qs, kb).astype(jnp.float32) * (Dh ** -0.5)
    qi = jnp.arange(BLK)[:, None]
    kj = jnp.arange(2 * BLK)[None, :]
    dist = qi + BLK - kj
    blk = jnp.arange(n_blk)[:, None, None]
    valid = (dist >= 0) & (dist <= steps) & (blk * BLK + kj >= BLK)
    s = jnp.where(valid, s, -jnp.inf)
    lse = jax.nn.logsumexp(s, axis=-1)
    p = jnp.exp(s - lse[..., None]).astype(v.dtype)
    o = jnp.einsum('bhrnqk,bhrnkd->bhrnqd', p, vb)
    o = o.transpose(0, 3, 4, 2, 1, 5).reshape(B, s_pad, H, Dh)[:, :S]
    lse = lse.transpose(0, 3, 4, 2, 1).reshape(B, s_pad, H)[:, :S]
    return o, lse


def dilated_mixture_attention(q, k, v):
    outs, lses = [], []
    for window, dilation in A_PATTERNS:
        o, l = dilated_window_attention(q, k, v, dilation, window // dilation)
        outs.append(o)
        lses.append(l)
    w = jax.nn.softmax(jnp.stack(lses), axis=0)
    return jnp.einsum('gbsh,gbshd->bshd', w, jnp.stack(outs).astype(jnp.float32)).astype(q.dtype)


def causal_attention(q, k, v, scale):
    B, H, S, Dk = q.shape
    Dv = v.shape[-1]
    nb = S // BLK
    qb = q.reshape(B, H, nb, BLK, Dk).transpose(2, 0, 1, 3, 4)
    kpos = jnp.arange(S)

    def one_block(args):
        qblk, idx = args
        s = jnp.einsum('bhqd,bhkd->bhqk', qblk, k).astype(jnp.float32) * scale
        qpos = idx * BLK + jnp.arange(BLK)
        s = jnp.where(kpos[None, :] <= qpos[:, None], s, -jnp.inf)
        p = jax.nn.softmax(s, axis=-1).astype(v.dtype)
        return jnp.einsum('bhqk,bhkd->bhqd', p, v)

    out = lax.map(one_block, (qb, jnp.arange(nb)))
    return out.transpose(1, 2, 0, 3, 4).reshape(B, H, S, Dv)


def mixer_ab(h, w_in, w_out, conv_w, conv_b, ln_g, ln_b):
    B, S, _ = h.shape
    z = h @ w_in
    qa, ka, va, u, g = jnp.split(
        z, [A_WIDTH, 2 * A_WIDTH, 3 * A_WIDTH, 3 * A_WIDTH + B_CHANNELS], axis=-1)
    heads = lambda t: t.reshape(B, S, A_HEADS, A_HEAD_DIM)
    ya = dilated_mixture_attention(heads(qa), heads(ka), heads(va)).reshape(B, S, A_WIDTH)
    glu = u * jax.nn.sigmoid(g)
    conv = lax.conv_general_dilated(
        glu, conv_w, window_strides=(1,), padding=[(B_CONV_WIDTH - 1, 0)],
        dimension_numbers=('NWC', 'WIO', 'NWC'), feature_group_count=B_CHANNELS) + conv_b
    yb = jax.nn.silu(layer_norm(conv, ln_g, ln_b))
    return jnp.concatenate([ya, yb], axis=-1) @ w_out


def mixer_cd(h, positions, layer_idx, w_in, w_out, lq1, lk1, lq2, lk2, subln_g,
             q_norm_g, kv_norm_g, w_uq, w_uk, w_uv):
    B, S, _ = h.shape
    z = h @ w_in
    o1 = C_WIDTH
    o2 = 2 * C_WIDTH
    o3 = 3 * C_WIDTH
    o4 = o3 + D_Q_RANK
    o5 = o4 + D_KV_RANK
    qc, kc, vc, cq, ckv, kr = jnp.split(z, [o1, o2, o3, o4, o5], axis=-1)

    def two_maps(t):
        t = t.reshape(B, S, C_HEADS, 2, C_HEAD_DIM)
        return t.transpose(0, 3, 2, 1, 4).reshape(B, 2 * C_HEADS, S, C_HEAD_DIM)
    vch = vc.reshape(B, S, C_HEADS, 2 * C_HEAD_DIM).transpose(0, 2, 1, 3)
    a = causal_attention(two_maps(qc), two_maps(kc), jnp.concatenate([vch, vch], axis=1),
                         C_HEAD_DIM ** -0.5)
    lam_init = 0.8 - 0.6 * math.exp(-0.3 * layer_idx)
    lam = jnp.exp(jnp.sum(lq1 * lk1)) - jnp.exp(jnp.sum(lq2 * lk2)) + lam_init
    yc = rms_norm(a[:, :C_HEADS] - lam * a[:, C_HEADS:], subln_g) * (1.0 - lam_init)
    yc = yc.transpose(0, 2, 1, 3).reshape(B, S, C_WIDTH)

    q = (rms_norm(cq, q_norm_g) @ w_uq).reshape(B, S, D_HEADS, D_NOPE_DIM + D_ROPE_DIM)
    qd = jnp.concatenate([q[..., :D_NOPE_DIM], rope(q[..., D_NOPE_DIM:], positions)], axis=-1)
    ckv = rms_norm(ckv, kv_norm_g)
    k_nope = (ckv @ w_uk).reshape(B, S, D_HEADS, D_NOPE_DIM)
    vd = (ckv @ w_uv).reshape(B, S, D_HEADS, D_V_DIM)
    k_rope = rope(kr[:, :, None, :], positions)
    kd = jnp.concatenate(
        [k_nope, jnp.broadcast_to(k_rope, (B, S, D_HEADS, D_ROPE_DIM))], axis=-1)
    yd = causal_attention(qd.transpose(0, 2, 1, 3), kd.transpose(0, 2, 1, 3),
                          vd.transpose(0, 2, 1, 3), (D_NOPE_DIM + D_ROPE_DIM) ** -0.5)
    yd = yd.transpose(0, 2, 1, 3).reshape(B, S, D_HEADS * D_V_DIM)
    return jnp.concatenate([yc, yd], axis=-1) @ w_out


def memory_cross_attention(h, mem, mem_norm_g, wq, wkv, wo):
    B, S, _ = h.shape
    M = mem.shape[1]
    q = (h @ wq).reshape(B, S, X_HEADS, X_HEAD_DIM)
    k, v = jnp.split(rms_norm(mem, mem_norm_g) @ wkv, 2, axis=-1)
    k = k.reshape(B, M, X_HEADS, X_HEAD_DIM)
    v = v.reshape(B, M, X_HEADS, X_HEAD_DIM)
    s = jnp.einsum('bshd,bmhd->bhsm', q, k).astype(jnp.float32) * (X_HEAD_DIM ** -0.5)
    p = jax.nn.softmax(s, axis=-1).astype(v.dtype)
    o = jnp.einsum('bhsm,bmhd->bshd', p, v).reshape(B, S, X_HEADS * X_HEAD_DIM)
    return o @ wo


def squared_relu_mlp(h, w1, w2):
    return jnp.square(jax.nn.relu(h @ w1)) @ w2


def setup_inputs(seed: int = 0) -> dict:
    key = jax.random.key(seed)
    keys = iter(jax.random.split(key, 64))
    nrm = lambda shape, scale: jax.random.normal(next(keys), shape, jnp.float32) * scale
    gain = lambda shape: 1.0 + nrm(shape, 0.02)
    L, E, O, D = DEPTH, N_EVEN, N_ODD, D_MODEL
    offset = jax.random.randint(next(keys), (BATCH, 1), 0, 4096, dtype=jnp.int32)
    return {
        'x': nrm((BATCH, SEQ, D), 1.0),
        'mem': nrm((BATCH, MEM_LEN, D), 1.0),
        'positions': offset + jnp.arange(SEQ, dtype=jnp.int32)[None, :],
        'norm_mix_g': gain((L, D)),
        'norm_cross_g': gain((L, D)),
        'norm_mem_g': gain((L, D)),
        'cross_wq': nrm((L, D, X_HEADS * X_HEAD_DIM), D ** -0.5),
        'cross_wkv': nrm((L, D, 2 * X_HEADS * X_HEAD_DIM), D ** -0.5),
        'cross_wo': nrm((L, X_HEADS * X_HEAD_DIM, D), (X_HEADS * X_HEAD_DIM) ** -0.5),
        'norm_mlp_g': gain((L, D)),
        'mlp_w1': nrm((L, D, D_FF), D ** -0.5),
        'mlp_w2': nrm((L, D_FF, D), D_FF ** -0.5),
        'ab_w_in': nrm((E, D, AB_IN), D ** -0.5),
        'ab_w_out': nrm((E, AB_OUT, D), AB_OUT ** -0.5),
        'ab_conv_w': nrm((E, B_CONV_WIDTH, 1, B_CHANNELS), B_CONV_WIDTH ** -0.5),
        'ab_conv_b': nrm((E, B_CHANNELS), 0.02),
        'ab_ln_g': gain((E, B_CHANNELS)),
        'ab_ln_b': nrm((E, B_CHANNELS), 0.02),
        'cd_w_in': nrm((O, D, CD_IN), D ** -0.5),
        'cd_w_out': nrm((O, CD_OUT, D), CD_OUT ** -0.5),
        'diff_lq1': nrm((O, C_HEAD_DIM), 0.1),
        'diff_lk1': nrm((O, C_HEAD_DIM), 0.1),
        'diff_lq2': nrm((O, C_HEAD_DIM), 0.1),
        'diff_lk2': nrm((O, C_HEAD_DIM), 0.1),
        'diff_subln_g': gain((O, 2 * C_HEAD_DIM)),
        'mla_q_norm_g': gain((O, D_Q_RANK)),
        'mla_kv_norm_g': gain((O, D_KV_RANK)),
        'mla_w_uq': nrm((O, D_Q_RANK, D_HEADS * (D_NOPE_DIM + D_ROPE_DIM)), D_Q_RANK ** -0.5),
        'mla_w_uk': nrm((O, D_KV_RANK, D_HEADS * D_NOPE_DIM), D_KV_RANK ** -0.5),
        'mla_w_uv': nrm((O, D_KV_RANK, D_HEADS * D_V_DIM), D_KV_RANK ** -0.5),
        'final_norm_g': gain((D,)),
    }


def reference(x, mem, positions, norm_mix_g, norm_cross_g, norm_mem_g, cross_wq, cross_wkv,
              cross_wo, norm_mlp_g, mlp_w1, mlp_w2, ab_w_in, ab_w_out, ab_conv_w, ab_conv_b,
              ab_ln_g, ab_ln_b, cd_w_in, cd_w_out, diff_lq1, diff_lk1, diff_lq2, diff_lk2,
              diff_subln_g, mla_q_norm_g, mla_kv_norm_g, mla_w_uq, mla_w_uk, mla_w_uv,
              final_norm_g):
    for i in range(DEPTH):
        j = i // 2
        h = rms_norm(x, norm_mix_g[i])
        if i % 2 == 0:
            x = x + mixer_ab(h, ab_w_in[j], ab_w_out[j], ab_conv_w[j], ab_conv_b[j],
                             ab_ln_g[j], ab_ln_b[j])
        else:
            x = x + mixer_cd(h, positions, i, cd_w_in[j], cd_w_out[j], diff_lq1[j], diff_lk1[j],
                             diff_lq2[j], diff_lk2[j], diff_subln_g[j], mla_q_norm_g[j],
                             mla_kv_norm_g[j], mla_w_uq[j], mla_w_uk[j], mla_w_uv[j])
        x = x + memory_cross_attention(rms_norm(x, norm_cross_g[i]), mem, norm_mem_g[i],
                                       cross_wq[i], cross_wkv[i], cross_wo[i])
        x = x + squared_relu_mlp(rms_norm(x, norm_mlp_g[i]), mlp_w1[i], mlp_w2[i])
    return rms_norm(x, final_norm_g)
```

```python
import functools
import math

import jax
import jax.numpy as jnp
from jax import lax
from jax.experimental import pallas as pl
from jax.experimental.pallas import tpu as pltpu

F32 = jnp.float32
BF16 = jnp.bfloat16

D_MODEL = 1024
EPS = 1e-6
NEG = -1e30

A_HEADS = 8
A_HEAD_DIM = 64
A_WIDTH = A_HEADS * A_HEAD_DIM
A_PATTERNS = ((128, 1), (512, 4), (2048, 16))
A_BLK = 128
B_CHANNELS = 512
B_CONV_WIDTH = 31
B_HALO = 32
C_HEADS = 4
C_HEAD_DIM = 64
C_WIDTH = C_HEADS * 2 * C_HEAD_DIM
D_HEADS = 8
D_NOPE_DIM = 64
D_ROPE_DIM = 32
D_V_DIM = 64
D_Q_RANK = 384
D_KV_RANK = 256
D_PAD_DIM = 128
ROPE_THETA = 10000.0
X_HEADS = 4
X_HEAD_DIM = 128
D_FF = 4 * D_MODEL
FF_CHUNK = 1024

ROW_TILE = 512
ATTN_TILE = 512
VMEM_LIMIT = 56 * 1024 * 1024


def _cparams(sem):
    return pltpu.CompilerParams(dimension_semantics=sem, vmem_limit_bytes=VMEM_LIMIT)


def _rms(x, g):
    return x * lax.rsqrt(jnp.mean(x * x, axis=-1, keepdims=True) + EPS) * g


def _dot(a, b):
    return jnp.dot(a, b, preferred_element_type=F32)


def _dot_t(a, b):
    return lax.dot_general(a, b, (((1,), (1,)), ((), ())), preferred_element_type=F32)


def _const_spec(shape):
    nd = len(shape)
    return pl.BlockSpec(shape, lambda *_: (0,) * nd)


def _ab_in_kernel(x_ref, g_ref, wqkv_ref, wu_ref, wg_ref, qkv_ref, glu_ref):
    n = _rms(x_ref[...], g_ref[...]).astype(BF16)
    qkv_ref[...] = _dot(n, wqkv_ref[...]).astype(BF16)
    u = _dot(n, wu_ref[...])
    gate = _dot(n, wg_ref[...])
    glu_ref[...] = u * jax.nn.sigmoid(gate)


def _ab_in(x2, g, w_in):
    t = x2.shape[0]
    wqkv = w_in[:, :3 * A_WIDTH].astype(BF16)
    wu = w_in[:, 3 * A_WIDTH:3 * A_WIDTH + B_CHANNELS].astype(BF16)
    wg = w_in[:, 3 * A_WIDTH + B_CHANNELS:].astype(BF16)
    return pl.pallas_call(
        _ab_in_kernel,
        grid=(t // ROW_TILE,),
        in_specs=[
            pl.BlockSpec((ROW_TILE, D_MODEL), lambda i: (i, 0)),
            _const_spec((1, D_MODEL)),
            _const_spec((D_MODEL, 3 * A_WIDTH)),
            _const_spec((D_MODEL, B_CHANNELS)),
            _const_spec((D_MODEL, B_CHANNELS)),
        ],
        out_specs=[
            pl.BlockSpec((ROW_TILE, 3 * A_WIDTH), lambda i: (i, 0)),
            pl.BlockSpec((ROW_TILE, B_CHANNELS), lambda i: (i, 0)),
        ],
        out_shape=[
            jax.ShapeDtypeStruct((t, 3 * A_WIDTH), BF16),
            jax.ShapeDtypeStruct((t, B_CHANNELS), F32),
        ],
        compiler_params=_cparams(("parallel",)),
        name="ab_in_proj",
    )(x2, g.reshape(1, D_MODEL), wqkv, wu, wg)


def _dilated_kernel(q_ref, kp_ref, kc_ref, vp_ref, vc_ref, o_ref, lse_ref, *, rows):
    first_tile = pl.program_id(2) == 0
    qi = lax.broadcasted_iota(jnp.int32, (A_BLK, 2 * A_BLK), 0)
    kj = lax.broadcasted_iota(jnp.int32, (A_BLK, 2 * A_BLK), 1)
    band = ((kj < A_BLK) & (kj >= qi)) | ((kj >= A_BLK) & (kj - A_BLK <= qi))
    scale = A_HEAD_DIM ** -0.5
    for i in range(rows // A_BLK):
        cur = slice(i * A_BLK, (i + 1) * A_BLK)
        if i == 0:
            k_prev, v_prev = kp_ref[0], vp_ref[0]
            valid = band & ((kj >= A_BLK) | jnp.logical_not(first_tile))
        else:
            prev = slice((i - 1) * A_BLK, i * A_BLK)
            k_prev, v_prev = kc_ref[0, prev, :], vc_ref[0, prev, :]
            valid = band
        k_all = jnp.concatenate([k_prev, kc_ref[0, cur, :]], axis=0)
        v_all = jnp.concatenate([v_prev, vc_ref[0, cur, :]], axis=0)
        q_all = q_ref[0, cur, :]
        for h in range(A_HEADS):
            cols = slice(h * A_HEAD_DIM, (h + 1) * A_HEAD_DIM)
            s = _dot_t(q_all[:, cols], k_all[:, cols]) * scale
            s = jnp.where(valid, s, NEG)
            m = jnp.max(s, axis=-1, keepdims=True)
            p = jnp.exp(s - m)
            l = jnp.sum(p, axis=-1, keepdims=True)
            o = _dot(p.astype(BF16), v_all[:, cols]) / l
            o_ref[0, cur, cols] = o.astype(BF16)
            lse_ref[0, cur, cols] = jnp.broadcast_to(m + jnp.log(l), (A_BLK, A_HEAD_DIM))


def _dilated_pattern(qkv, dilation):
    b, s, _ = qkv.shape
    sub = s // dilation
    rows = min(sub, 512)
    assert sub % rows == 0 and rows % A_BLK == 0
    view = qkv.reshape(b, sub, dilation * 3 * A_WIDTH)
    blocks_per_tile = rows // A_BLK

    def cur_spec(part):
        return pl.BlockSpec((1, rows, A_WIDTH), lambda bi, r, n: (bi, n, 3 * r + part))

    def prev_spec(part):
        return pl.BlockSpec(
            (1, A_BLK, A_WIDTH),
            lambda bi, r, n: (bi, jnp.maximum(n * blocks_per_tile - 1, 0), 3 * r + part))

    out_spec = pl.BlockSpec((1, rows, A_WIDTH), lambda bi, r, n: (bi, n, r))
    o, lse = pl.pallas_call(
        functools.partial(_dilated_kernel, rows=rows),
        grid=(b, dilation, sub // rows),
        in_specs=[cur_spec(0), prev_spec(1), cur_spec(1), prev_spec(2), cur_spec(2)],
        out_specs=[out_spec, out_spec],
        out_shape=[
            jax.ShapeDtypeStruct((b, sub, dilation * A_WIDTH), BF16),
            jax.ShapeDtypeStruct((b, sub, dilation * A_WIDTH), F32),
        ],
        compiler_params=_cparams(("parallel", "parallel", "arbitrary")),
        name=f"dilated_attn_d{dilation}",
    )(view, view, view, view, view)
    return o.reshape(b * s, A_WIDTH), lse.reshape(b * s, A_WIDTH)


CONV_ROWS = 32
SUBLANES = 8


def _conv_kernel(halo_ref, cur_ref, w_ref, cb_ref, g_ref, b_ref, out_ref, buf_ref, *, rows):
    first_tile = pl.program_id(1) == 0
    halo = halo_ref[0]
    buf_ref[0:B_HALO, :] = jnp.where(first_tile, jnp.zeros_like(halo), halo)
    buf_ref[B_HALO:B_HALO + rows, :] = cur_ref[0]
    lead = B_HALO - (B_CONV_WIDTH - 1)

    def chunk(c, carry):
        base = pl.multiple_of(c * CONV_ROWS, CONV_ROWS)
        win = buf_ref[pl.ds(base, CONV_ROWS + B_HALO), :]
        shifted = [win] + [win[r:r + CONV_ROWS + B_HALO - SUBLANES, :] for r in range(1, SUBLANES)]
        acc = jnp.zeros((CONV_ROWS, B_CHANNELS), F32)
        for tap in range(B_CONV_WIDTH):
            r, a = (lead + tap) % SUBLANES, (lead + tap) // SUBLANES * SUBLANES
            acc = acc + shifted[r][a:a + CONV_ROWS, :] * w_ref[tap:tap + 1, :]
        y = acc + cb_ref[...]
        mu = jnp.mean(y, axis=-1, keepdims=True)
        yc = y - mu
        var = jnp.mean(yc * yc, axis=-1, keepdims=True)
        z = yc * lax.rsqrt(var + EPS) * g_ref[...] + b_ref[...]
        out_ref[0, pl.ds(base, CONV_ROWS), :] = (z * jax.nn.sigmoid(z)).astype(BF16)
        return carry

    lax.fori_loop(0, rows // CONV_ROWS, chunk, 0)


def _conv_module(glu, conv_w, conv_b, ln_g, ln_b):
    b, s, c = glu.shape
    rows = ROW_TILE
    halo_blocks = rows // B_HALO
    vec = lambda a: a.reshape(1, c)
    return pl.pallas_call(
        functools.partial(_conv_kernel, rows=rows),
        grid=(b, s // rows),
        in_specs=[
            pl.BlockSpec((1, B_HALO, c), lambda bi, n: (bi, jnp.maximum(n * halo_blocks - 1, 0), 0)),
            pl.BlockSpec((1, rows, c), lambda bi, n: (bi, n, 0)),
            _const_spec((B_CONV_WIDTH, c)),
            _const_spec((1, c)), _const_spec((1, c)), _const_spec((1, c)),
        ],
        out_specs=pl.BlockSpec((1, rows, c), lambda bi, n: (bi, n, 0)),
        out_shape=jax.ShapeDtypeStruct((b, s, c), BF16),
        scratch_shapes=[pltpu.VMEM((B_HALO + rows, c), F32)],
        compiler_params=_cparams(("parallel", "arbitrary")),
        name="conformer_conv",
    )(glu, glu, conv_w.reshape(B_CONV_WIDTH, c), vec(conv_b), vec(ln_g), vec(ln_b))


def _ab_out_kernel(x_ref, o1_ref, o2_ref, o3_ref, l1_ref, l2_ref, l3_ref, yb_ref, wa_ref, wb_ref,
                   out_ref):
    l1, l2, l3 = l1_ref[...], l2_ref[...], l3_ref[...]
    m = jnp.maximum(jnp.maximum(l1, l2), l3)
    e1, e2, e3 = jnp.exp(l1 - m), jnp.exp(l2 - m), jnp.exp(l3 - m)
    ya = (e1 * o1_ref[...].astype(F32) + e2 * o2_ref[...].astype(F32)
          + e3 * o3_ref[...].astype(F32)) / (e1 + e2 + e3)
    out_ref[...] = (x_ref[...] + _dot(ya.astype(BF16), wa_ref[...])
                    + _dot(yb_ref[...], wb_ref[...]))


def _ab_out(x2, outs, lses, yb, w_out):
    t = x2.shape[0]
    row = lambda w: pl.BlockSpec((ROW_TILE, w), lambda i: (i, 0))
    return pl.pallas_call(
        _ab_out_kernel,
        grid=(t // ROW_TILE,),
        in_specs=[row(D_MODEL)] + [row(A_WIDTH)] * 6 + [row(B_CHANNELS),
                  _const_spec((A_WIDTH, D_MODEL)), _const_spec((B_CHANNELS, D_MODEL))],
        out_specs=row(D_MODEL),
        out_shape=jax.ShapeDtypeStruct((t, D_MODEL), F32),
        compiler_params=_cparams(("parallel",)),
        name="ab_out_proj",
    )(x2, *outs, *lses, yb, w_out[:A_WIDTH].astype(BF16), w_out[A_WIDTH:].astype(BF16))


def _mem_kv_kernel(mem_ref, g_ref, wk_ref, wv_ref, k_ref, v_ref):
    n = _rms(mem_ref[...], g_ref[...]).astype(BF16)
    k_ref[...] = _dot(n, wk_ref[...]).astype(BF16)
    v_ref[...] = _dot(n, wv_ref[...]).astype(BF16)


def _mem_kv(mem2, g, wkv):
    rows = mem2.shape[0]
    w = X_HEADS * X_HEAD_DIM
    return pl.pallas_call(
        _mem_kv_kernel,
        grid=(1,),
        in_specs=[_const_spec((rows, D_MODEL)), _const_spec((1, D_MODEL)),
                  _const_spec((D_MODEL, w)), _const_spec((D_MODEL, w))],
        out_specs=[_const_spec((rows, w)), _const_spec((rows, w))],
        out_shape=[jax.ShapeDtypeStruct((rows, w), BF16)] * 2,
        compiler_params=_cparams(("arbitrary",)),
        name="mem_kv_proj",
    )(mem2, g.reshape(1, D_MODEL), wkv[:, :w].astype(BF16), wkv[:, w:].astype(BF16))


def _cross_kernel(x_ref, g_ref, wq_ref, k_ref, v_ref, wo_ref, out_ref):
    x = x_ref[0]
    n = _rms(x, g_ref[...]).astype(BF16)
    q = (_dot(n, wq_ref[...]) * (X_HEAD_DIM ** -0.5)).astype(BF16)
    heads = []
    for h in range(X_HEADS):
        cols = slice(h * X_HEAD_DIM, (h + 1) * X_HEAD_DIM)
        s = _dot_t(q[:, cols], k_ref[0, :, cols])
        m = jnp.max(s, axis=-1, keepdims=True)
        p = jnp.exp(s - m)
        l = jnp.sum(p, axis=-1, keepdims=True)
        heads.append((_dot(p.astype(BF16), v_ref[0, :, cols]) / l).astype(BF16))
    o = jnp.concatenate(heads, axis=-1)
    out_ref[0] = x + _dot(o, wo_ref[...])


def _cross_attention(x3, g, wq, k, v, wo):
    b, s, _ = x3.shape
    m = k.shape[1]
    w = X_HEADS * X_HEAD_DIM
    return pl.pallas_call(
        _cross_kernel,
        grid=(b, s // ROW_TILE),
        in_specs=[
            pl.BlockSpec((1, ROW_TILE, D_MODEL), lambda bi, i: (bi, i, 0)),
            _const_spec((1, D_MODEL)),
            _const_spec((D_MODEL, w)),
            pl.BlockSpec((1, m, w), lambda bi, i: (bi, 0, 0)),
            pl.BlockSpec((1, m, w), lambda bi, i: (bi, 0, 0)),
            _const_spec((w, D_MODEL)),
        ],
        out_specs=pl.BlockSpec((1, ROW_TILE, D_MODEL), lambda bi, i: (bi, i, 0)),
        out_shape=jax.ShapeDtypeStruct((b, s, D_MODEL), F32),
        compiler_params=_cparams(("parallel", "parallel")),
        name="cross_attn",
    )(x3, g.reshape(1, D_MODEL), wq.astype(BF16), k, v, wo.astype(BF16))


def _mlp_kernel(x_ref, g_ref, w1_ref, w2_ref, gf_ref, out_ref, *, final_norm):
    x = x_ref[...]
    n = _rms(x, g_ref[...]).astype(BF16)
    acc = x
    for c in range(D_FF // FF_CHUNK):
        cols = slice(c * FF_CHUNK, (c + 1) * FF_CHUNK)
        h = jnp.maximum(_dot(n, w1_ref[:, cols]), 0.0)
        acc = acc + _dot((h * h).astype(BF16), w2_ref[cols, :])
    if final_norm:
        acc = _rms(acc, gf_ref[...])
    out_ref[...] = acc


def _mlp(x2, g, w1, w2, gf, final_norm):
    t = x2.shape[0]
    return pl.pallas_call(
        functools.partial(_mlp_kernel, final_norm=final_norm),
        grid=(t // ROW_TILE,),
        in_specs=[
            pl.BlockSpec((ROW_TILE, D_MODEL), lambda i: (i, 0)),
            _const_spec((1, D_MODEL)),
            _const_spec((D_MODEL, D_FF)),
            _const_spec((D_FF, D_MODEL)),
            _const_spec((1, D_MODEL)),
        ],
        out_specs=pl.BlockSpec((ROW_TILE, D_MODEL), lambda i: (i, 0)),
        out_shape=jax.ShapeDtypeStruct((t, D_MODEL), F32),
        compiler_params=_cparams(("parallel",)),
        name="mlp_final" if final_norm else "mlp",
    )(x2, g.reshape(1, D_MODEL), w1.astype(BF16), w2.astype(BF16), gf.reshape(1, D_MODEL))


def _rope_tile(x, cos_t, sin_lo, sin_hi):
    half = D_ROPE_DIM // 2
    return (x * cos_t + pltpu.roll(x, D_PAD_DIM - half, 1) * sin_lo
            + pltpu.roll(x, half, 1) * sin_hi)


def _cd_in_kernel(x_ref, pos_ref, g_ref, freq_ref, wqkv_ref, wcq_ref, wckv_ref, wkr_ref,
                  qng_ref, kvng_ref, wuq_ref, wuk_ref, wuv_ref,
                  qc_ref, kc_ref, vc_ref, qd_ref, kd_ref, vd_ref):
    n = _rms(x_ref[0], g_ref[...]).astype(BF16)
    qkv = _dot(n, wqkv_ref[...])
    qc_ref[0] = (qkv[:, :C_WIDTH] * (C_HEAD_DIM ** -0.5)).astype(BF16)
    kc_ref[0] = qkv[:, C_WIDTH:2 * C_WIDTH].astype(BF16)
    vc_ref[0] = qkv[:, 2 * C_WIDTH:].astype(BF16)

    ang = pos_ref[0].astype(F32) * freq_ref[...]
    cos_t, sin_t = jnp.cos(ang), jnp.sin(ang)
    lane = lax.broadcasted_iota(jnp.int32, ang.shape, 1)
    split = D_NOPE_DIM + D_ROPE_DIM // 2
    sin_lo = jnp.where(lane < split, -sin_t, 0.0)
    sin_hi = jnp.where(lane >= split, sin_t, 0.0)

    cq = _rms(_dot(n, wcq_ref[...]), qng_ref[...]).astype(BF16)
    q = _dot(cq, wuq_ref[...])
    ckv = _rms(_dot(n, wckv_ref[...]), kvng_ref[...]).astype(BF16)
    k_nope = _dot(ckv, wuk_ref[...])
    vd_ref[0] = _dot(ckv, wuv_ref[...]).astype(BF16)
    k_rope = _rope_tile(_dot(n, wkr_ref[...]), cos_t, sin_lo, sin_hi)
    scale = (D_NOPE_DIM + D_ROPE_DIM) ** -0.5
    for h in range(D_HEADS):
        cols = slice(h * D_PAD_DIM, (h + 1) * D_PAD_DIM)
        qd_ref[0, :, cols] = (_rope_tile(q[:, cols], cos_t, sin_lo, sin_hi) * scale).astype(BF16)
        kd_ref[0, :, cols] = (k_nope[:, cols] + k_rope).astype(BF16)


def _pad_heads(w, heads, width):
    k = w.shape[0]
    w = w.reshape(k, heads, width)
    return jnp.pad(w, ((0, 0), (0, 0), (0, D_PAD_DIM - width))).reshape(k, heads * D_PAD_DIM)


def _cd_in(x3, positions, g, w_in, q_norm_g, kv_norm_g, w_uq, w_uk, w_uv):
    b, s, _ = x3.shape
    o3 = 3 * C_WIDTH
    o4 = o3 + D_Q_RANK
    o5 = o4 + D_KV_RANK
    wqkv = w_in[:, :o3].astype(BF16)
    wcq = w_in[:, o3:o4].astype(BF16)
    wckv = w_in[:, o4:o5].astype(BF16)
    wkr = jnp.pad(w_in[:, o5:], ((0, 0), (D_NOPE_DIM, D_PAD_DIM - D_NOPE_DIM - D_ROPE_DIM))).astype(BF16)
    wuq = _pad_heads(w_uq, D_HEADS, D_NOPE_DIM + D_ROPE_DIM).astype(BF16)
    wuk = _pad_heads(w_uk, D_HEADS, D_NOPE_DIM).astype(BF16)
    half = D_ROPE_DIM // 2
    inv_freq = ROPE_THETA ** (-jnp.arange(half, dtype=F32) / half)
    freq = jnp.zeros((1, D_PAD_DIM), F32).at[0, D_NOPE_DIM:D_NOPE_DIM + D_ROPE_DIM].set(
        jnp.concatenate([inv_freq, inv_freq]))
    dw = D_HEADS * D_PAD_DIM
    row = lambda w: pl.BlockSpec((1, ROW_TILE, w), lambda bi, i: (bi, i, 0))
    return pl.pallas_call(
        _cd_in_kernel,
        grid=(b, s // ROW_TILE),
        in_specs=[
            row(D_MODEL), row(1), _const_spec((1, D_MODEL)), _const_spec((1, D_PAD_DIM)),
            _const_spec((D_MODEL, o3)), _const_spec((D_MODEL, D_Q_RANK)),
            _const_spec((D_MODEL, D_KV_RANK)), _const_spec((D_MODEL, D_PAD_DIM)),
            _const_spec((1, D_Q_RANK)), _const_spec((1, D_KV_RANK)),
            _const_spec((D_Q_RANK, dw)), _const_spec((D_KV_RANK, dw)),
            _const_spec((D_KV_RANK, D_HEADS * D_V_DIM)),
        ],
        out_specs=[row(C_WIDTH), row(C_WIDTH), row(C_WIDTH), row(dw), row(dw),
                   row(D_HEADS * D_V_DIM)],
        out_shape=[
            jax.ShapeDtypeStruct((b, s, C_WIDTH), BF16),
            jax.ShapeDtypeStruct((b, s, C_WIDTH), BF16),
            jax.ShapeDtypeStruct((b, s, C_WIDTH), BF16),
            jax.ShapeDtypeStruct((b, s, dw), BF16),
            jax.ShapeDtypeStruct((b, s, dw), BF16),
            jax.ShapeDtypeStruct((b, s, D_HEADS * D_V_DIM), BF16),
        ],
        compiler_params=_cparams(("parallel", "parallel")),
        name="cd_in_proj",
    )(x3, positions.reshape(b, s, 1), g.reshape(1, D_MODEL), freq, wqkv, wcq, wckv, wkr,
      q_norm_g.reshape(1, D_Q_RANK), kv_norm_g.reshape(1, D_KV_RANK), wuq, wuk,
      w_uv.astype(BF16))


def _causal_kernel(q_ref, k_ref, v_ref, *rest, mode, lam_init):
    if mode == "diff":
        lq1_ref, lk1_ref, lq2_ref, lk2_ref, g_ref, out_ref = rest
    else:
        (out_ref,) = rest
    t = ATTN_TILE
    qi = pl.program_id(2)
    q = q_ref[0]
    lane = lax.broadcasted_iota(jnp.int32, (t, 128), 1)
    if mode == "diff":
        zero = jnp.zeros_like(q)
        qs = (jnp.where(lane < C_HEAD_DIM, q, zero), jnp.where(lane >= C_HEAD_DIM, q, zero))
    else:
        qs = (q[:, :D_PAD_DIM], q[:, D_PAD_DIM:])

    def keys(j, rows):
        if mode == "diff":
            return k_ref[0, rows, :]
        return k_ref[0, rows, j * D_PAD_DIM:(j + 1) * D_PAD_DIM]

    def step(kb, carry, masked):
        rows = pl.ds(pl.multiple_of(kb * t, t), t)
        v = v_ref[0, rows, :]
        new = []
        for j in range(2):
            m, l, acc = carry[j]
            s = _dot_t(qs[j], keys(j, rows))
            if masked:
                r = lax.broadcasted_iota(jnp.int32, (t, t), 0)
                c = lax.broadcasted_iota(jnp.int32, (t, t), 1)
                s = jnp.where(c <= r, s, NEG)
            m_new = jnp.maximum(m, jnp.max(s, axis=-1, keepdims=True))
            alpha = jnp.exp(m - m_new)
            p = jnp.exp(s - m_new)
            l = alpha * l + jnp.sum(p, axis=-1, keepdims=True)
            acc = alpha * acc + _dot(p.astype(BF16), v)
            new.append((m_new, l, acc))
        return tuple(new)

    init = tuple((jnp.full((t, 1), NEG, F32), jnp.zeros((t, 1), F32), jnp.zeros((t, 128), F32))
                 for _ in range(2))
    carry = lax.fori_loop(0, qi, lambda kb, c: step(kb, c, False), init)
    carry = step(qi, carry, True)
    a1 = carry[0][2] / carry[0][1]
    a2 = carry[1][2] / carry[1][1]
    if mode == "diff":
        lam = (jnp.exp(jnp.sum(lq1_ref[...] * lk1_ref[...], axis=-1, keepdims=True))
               - jnp.exp(jnp.sum(lq2_ref[...] * lk2_ref[...], axis=-1, keepdims=True)) + lam_init)
        out = _rms(a1 - lam * a2, g_ref[...]) * (1.0 - lam_init)
    else:
        out = jnp.where(lane < D_V_DIM, a1, a2)
    out_ref[0] = out.astype(BF16)


def _causal_attention(q, k, v, mode, extras=(), lam_init=0.0):
    b, s, _ = q.shape
    t = ATTN_TILE
    qw = 128 if mode == "diff" else 2 * D_PAD_DIM
    groups = q.shape[2] // qw
    in_specs = [
        pl.BlockSpec((1, t, qw), lambda bi, h, i: (bi, i, h)),
        pl.BlockSpec((1, s, qw), lambda bi, h, i: (bi, 0, h)),
        pl.BlockSpec((1, s, 128), lambda bi, h, i: (bi, 0, h)),
    ] + [_const_spec(e.shape) for e in extras]
    return pl.pallas_call(
        functools.partial(_causal_kernel, mode=mode, lam_init=lam_init),
        grid=(b, groups, s // t),
        in_specs=in_specs,
        out_specs=pl.BlockSpec((1, t, 128), lambda bi, h, i: (bi, i, h)),
        out_shape=jax.ShapeDtypeStruct((b, s, groups * 128), BF16),
        compiler_params=_cparams(("parallel", "parallel", "arbitrary")),
        name=f"causal_attn_{mode}",
    )(q, k, v, *extras)


def _cd_out_kernel(x_ref, yc_ref, yd_ref, wc_ref, wd_ref, out_ref):
    out_ref[...] = x_ref[...] + _dot(yc_ref[...], wc_ref[...]) + _dot(yd_ref[...], wd_ref[...])


def _cd_out(x2, yc, yd, w_out):
    t = x2.shape[0]
    row = lambda w: pl.BlockSpec((ROW_TILE, w), lambda i: (i, 0))
    dvw = D_HEADS * D_V_DIM
    return pl.pallas_call(
        _cd_out_kernel,
        grid=(t // ROW_TILE,),
        in_specs=[row(D_MODEL), row(C_WIDTH), row(dvw),
                  _const_spec((C_WIDTH, D_MODEL)), _const_spec((dvw, D_MODEL))],
        out_specs=row(D_MODEL),
        out_shape=jax.ShapeDtypeStruct((t, D_MODEL), F32),
        compiler_params=_cparams(("parallel",)),
        name="cd_out_proj",
    )(x2, yc, yd, w_out[:C_WIDTH].astype(BF16), w_out[C_WIDTH:].astype(BF16))


def kernel(x, mem, positions, norm_mix_g, norm_cross_g, norm_mem_g, cross_wq, cross_wkv, cross_wo,
           norm_mlp_g, mlp_w1, mlp_w2, ab_w_in, ab_w_out, ab_conv_w, ab_conv_b, ab_ln_g, ab_ln_b,
           cd_w_in, cd_w_out, diff_lq1, diff_lk1, diff_lq2, diff_lk2, diff_subln_g, mla_q_norm_g,
           mla_kv_norm_g, mla_w_uq, mla_w_uk, mla_w_uv, final_norm_g):
    b, s, d = x.shape
    t = b * s
    mem_len = mem.shape[1]
    depth = norm_mix_g.shape[0]
    xw = X_HEADS * X_HEAD_DIM
    x2 = x.reshape(t, d)
    for i in range(depth):
        j = i // 2
        if i % 2 == 0:
            qkv, glu = _ab_in(x2, norm_mix_g[i], ab_w_in[j])
            qkv3 = qkv.reshape(b, s, 3 * A_WIDTH)
            outs, lses = zip(*[_dilated_pattern(qkv3, dil) for _, dil in A_PATTERNS])
            yb = _conv_module(glu.reshape(b, s, B_CHANNELS), ab_conv_w[j], ab_conv_b[j],
                              ab_ln_g[j], ab_ln_b[j])
            x2 = _ab_out(x2, outs, lses, yb.reshape(t, B_CHANNELS), ab_w_out[j])
        else:
            qc, kc, vc, qd, kd, vd = _cd_in(
                x2.reshape(b, s, d), positions, norm_mix_g[i], cd_w_in[j], mla_q_norm_g[j],
                mla_kv_norm_g[j], mla_w_uq[j], mla_w_uk[j], mla_w_uv[j])
            lam_init = 0.8 - 0.6 * math.exp(-0.3 * i)
            vec = lambda a: a.reshape(1, -1)
            yc = _causal_attention(
                qc, kc, vc, "diff",
                extras=(vec(diff_lq1[j]), vec(diff_lk1[j]), vec(diff_lq2[j]), vec(diff_lk2[j]),
                        vec(diff_subln_g[j])),
                lam_init=lam_init)
            yd = _causal_attention(qd, kd, vd, "mla")
            x2 = _cd_out(x2, yc.reshape(t, C_WIDTH), yd.reshape(t, D_HEADS * D_V_DIM), cd_w_out[j])
        km, vm = _mem_kv(mem.reshape(b * mem_len, d), norm_mem_g[i], cross_wkv[i])
        x2 = _cross_attention(x2.reshape(b, s, d), norm_cross_g[i], cross_wq[i],
                              km.reshape(b, mem_len, xw), vm.reshape(b, mem_len, xw),
                              cross_wo[i]).reshape(t, d)
        x2 = _mlp(x2, norm_mlp_g[i], mlp_w1[i], mlp_w2[i], final_norm_g, i == depth - 1)
    return x2.reshape(b, s, d)
```

```python
import functools
import math

import jax
import jax.numpy as jnp
from jax import lax
from jax.experimental import pallas as pl
from jax.experimental.pallas import tpu as pltpu

F32 = jnp.float32
BF16 = jnp.bfloat16

D_MODEL = 1024
EPS = 1e-6
LOG2E = math.log2(math.e)
NEG = -1e30

A_HEADS = 8
A_HEAD_DIM = 64
A_WIDTH = A_HEADS * A_HEAD_DIM
A_PATTERNS = ((128, 1), (512, 4), (2048, 16))
A_BLK = 128
B_CHANNELS = 512
B_CONV_WIDTH = 31
B_HALO = 32
C_HEADS = 4
C_HEAD_DIM = 64
C_WIDTH = C_HEADS * 2 * C_HEAD_DIM
D_HEADS = 8
D_NOPE_DIM = 64
D_ROPE_DIM = 32
D_V_DIM = 64
D_Q_RANK = 384
D_KV_RANK = 256
D_PAD_DIM = 128
ROPE_THETA = 10000.0
X_HEADS = 4
X_HEAD_DIM = 128
D_FF = 4 * D_MODEL
FF_CHUNK = 1024

ROW_TILE = 512
ATTN_TILE = ROW_TILE
VMEM_LIMIT = 56 * 1024 * 1024


def _cparams(sem):
    return pltpu.CompilerParams(dimension_semantics=sem, vmem_limit_bytes=VMEM_LIMIT)


def _rms(x, g):
    return x * lax.rsqrt(jnp.mean(x * x, axis=-1, keepdims=True) + EPS) * g


def _dot(a, b):
    return jnp.dot(a, b, preferred_element_type=F32)


def _dot_t(a, b):
    return lax.dot_general(a, b, (((1,), (1,)), ((), ())), preferred_element_type=F32)


def _const_spec(shape):
    nd = len(shape)
    return pl.BlockSpec(shape, lambda *_: (0,) * nd)


def _ab_in_kernel(x_ref, g_ref, wqkv_ref, wu_ref, wg_ref, qkv_ref, glu_ref):
    n = _rms(x_ref[...], g_ref[...]).astype(BF16)
    qkv_ref[...] = _dot(n, wqkv_ref[...]).astype(BF16)
    u = _dot(n, wu_ref[...])
    gate = _dot(n, wg_ref[...])
    glu_ref[...] = u * jax.nn.sigmoid(gate)


def _ab_in(x2, g, w_in):
    t = x2.shape[0]
    wqkv = w_in[:, :3 * A_WIDTH].astype(BF16)
    wu = w_in[:, 3 * A_WIDTH:3 * A_WIDTH + B_CHANNELS].astype(BF16)
    wg = w_in[:, 3 * A_WIDTH + B_CHANNELS:].astype(BF16)
    return pl.pallas_call(
        _ab_in_kernel,
        grid=(t // ROW_TILE,),
        in_specs=[
            pl.BlockSpec((ROW_TILE, D_MODEL), lambda i: (i, 0)),
            _const_spec((1, D_MODEL)),
            _const_spec((D_MODEL, 3 * A_WIDTH)),
            _const_spec((D_MODEL, B_CHANNELS)),
            _const_spec((D_MODEL, B_CHANNELS)),
        ],
        out_specs=[
            pl.BlockSpec((ROW_TILE, 3 * A_WIDTH), lambda i: (i, 0)),
            pl.BlockSpec((ROW_TILE, B_CHANNELS), lambda i: (i, 0)),
        ],
        out_shape=[
            jax.ShapeDtypeStruct((t, 3 * A_WIDTH), BF16),
            jax.ShapeDtypeStruct((t, B_CHANNELS), F32),
        ],
        compiler_params=_cparams(("parallel",)),
        name="ab_in_proj",
    )(x2, g.reshape(1, D_MODEL), wqkv, wu, wg)


def _dilated_kernel(q_ref, kp_ref, kc_ref, vp_ref, vc_ref, o_ref, lse_ref, *, rows):
    first_tile = pl.program_id(2) == 0
    qi = lax.broadcasted_iota(jnp.int32, (A_BLK, 2 * A_BLK), 0)
    kj = lax.broadcasted_iota(jnp.int32, (A_BLK, 2 * A_BLK), 1)
    band = ((kj < A_BLK) & (kj >= qi)) | ((kj >= A_BLK) & (kj - A_BLK <= qi))
    scale = A_HEAD_DIM ** -0.5
    for i in range(rows // A_BLK):
        cur = slice(i * A_BLK, (i + 1) * A_BLK)
        if i == 0:
            k_prev, v_prev = kp_ref[0], vp_ref[0]
            valid = band & ((kj >= A_BLK) | jnp.logical_not(first_tile))
        else:
            prev = slice((i - 1) * A_BLK, i * A_BLK)
            k_prev, v_prev = kc_ref[0, prev, :], vc_ref[0, prev, :]
            valid = band
        k_all = jnp.concatenate([k_prev, kc_ref[0, cur, :]], axis=0)
        v_all = jnp.concatenate([v_prev, vc_ref[0, cur, :]], axis=0)
        q_all = q_ref[0, cur, :]
        for h in range(A_HEADS):
            cols = slice(h * A_HEAD_DIM, (h + 1) * A_HEAD_DIM)
            s = _dot_t(q_all[:, cols], k_all[:, cols]) * scale
            s = jnp.where(valid, s, NEG)
            m = jnp.max(s, axis=-1, keepdims=True)
            p = jnp.exp(s - m)
            l = jnp.sum(p, axis=-1, keepdims=True)
            o = _dot(p.astype(BF16), v_all[:, cols]) / l
            o_ref[0, cur, cols] = o.astype(BF16)
            lse_ref[0, cur, cols] = jnp.broadcast_to(m + jnp.log(l), (A_BLK, A_HEAD_DIM))


def _dilated_pattern(qkv, dilation):
    b, s, _ = qkv.shape
    sub = s // dilation
    rows = min(sub, 512)
    assert sub % rows == 0 and rows % A_BLK == 0
    view = qkv.reshape(b, sub, dilation * 3 * A_WIDTH)
    blocks_per_tile = rows // A_BLK

    def cur_spec(part):
        return pl.BlockSpec((1, rows, A_WIDTH), lambda bi, r, n: (bi, n, 3 * r + part))

    def prev_spec(part):
        return pl.BlockSpec(
            (1, A_BLK, A_WIDTH),
            lambda bi, r, n: (bi, jnp.maximum(n * blocks_per_tile - 1, 0), 3 * r + part))

    out_spec = pl.BlockSpec((1, rows, A_WIDTH), lambda bi, r, n: (bi, n, r))
    o, lse = pl.pallas_call(
        functools.partial(_dilated_kernel, rows=rows),
        grid=(b, dilation, sub // rows),
        in_specs=[cur_spec(0), prev_spec(1), cur_spec(1), prev_spec(2), cur_spec(2)],
        out_specs=[out_spec, out_spec],
        out_shape=[
            jax.ShapeDtypeStruct((b, sub, dilation * A_WIDTH), BF16),
            jax.ShapeDtypeStruct((b, sub, dilation * A_WIDTH), F32),
        ],
        compiler_params=_cparams(("parallel", "parallel", "arbitrary")),
        name=f"dilated_attn_d{dilation}",
    )(view, view, view, view, view)
    return o.reshape(b * s, A_WIDTH), lse.reshape(b * s, A_WIDTH)


CONV_ROWS = 32
SUBLANES = 8


def _conv_kernel(halo_ref, cur_ref, w_ref, cb_ref, g_ref, b_ref, out_ref, buf_ref, *, rows):
    first_tile = pl.program_id(1) == 0
    halo = halo_ref[0]
    buf_ref[0:B_HALO, :] = jnp.where(first_tile, jnp.zeros_like(halo), halo)
    buf_ref[B_HALO:B_HALO + rows, :] = cur_ref[0]
    lead = B_HALO - (B_CONV_WIDTH - 1)

    def chunk(c, carry):
        base = pl.multiple_of(c * CONV_ROWS, CONV_ROWS)
        win = buf_ref[pl.ds(base, CONV_ROWS + B_HALO), :]
        shifted = [win] + [win[r:r + CONV_ROWS + B_HALO - SUBLANES, :] for r in range(1, SUBLANES)]
        acc = jnp.zeros((CONV_ROWS, B_CHANNELS), F32)
        for tap in range(B_CONV_WIDTH):
            r, a = (lead + tap) % SUBLANES, (lead + tap) // SUBLANES * SUBLANES
            acc = acc + shifted[r][a:a + CONV_ROWS, :] * w_ref[tap:tap + 1, :]
        y = acc + cb_ref[...]
        mu = jnp.mean(y, axis=-1, keepdims=True)
        yc = y - mu
        var = jnp.mean(yc * yc, axis=-1, keepdims=True)
        z = yc * lax.rsqrt(var + EPS) * g_ref[...] + b_ref[...]
        out_ref[0, pl.ds(base, CONV_ROWS), :] = (z * jax.nn.sigmoid(z)).astype(BF16)
        return carry

    lax.fori_loop(0, rows // CONV_ROWS, chunk, 0)


def _conv_module(glu, conv_w, conv_b, ln_g, ln_b):
    b, s, c = glu.shape
    rows = ROW_TILE
    halo_blocks = rows // B_HALO
    vec = lambda a: a.reshape(1, c)
    return pl.pallas_call(
        functools.partial(_conv_kernel, rows=rows),
        grid=(b, s // rows),
        in_specs=[
            pl.BlockSpec((1, B_HALO, c), lambda bi, n: (bi, jnp.maximum(n * halo_blocks - 1, 0), 0)),
            pl.BlockSpec((1, rows, c), lambda bi, n: (bi, n, 0)),
            _const_spec((B_CONV_WIDTH, c)),
            _const_spec((1, c)), _const_spec((1, c)), _const_spec((1, c)),
        ],
        out_specs=pl.BlockSpec((1, rows, c), lambda bi, n: (bi, n, 0)),
        out_shape=jax.ShapeDtypeStruct((b, s, c), BF16),
        scratch_shapes=[pltpu.VMEM((B_HALO + rows, c), F32)],
        compiler_params=_cparams(("parallel", "arbitrary")),
        name="conformer_conv",
    )(glu, glu, conv_w.reshape(B_CONV_WIDTH, c), vec(conv_b), vec(ln_g), vec(ln_b))


def _ab_out_kernel(x_ref, o1_ref, o2_ref, o3_ref, l1_ref, l2_ref, l3_ref, yb_ref, wa_ref, wb_ref,
                   out_ref):
    l1, l2, l3 = l1_ref[...], l2_ref[...], l3_ref[...]
    m = jnp.maximum(jnp.maximum(l1, l2), l3)
    e1, e2, e3 = jnp.exp(l1 - m), jnp.exp(l2 - m), jnp.exp(l3 - m)
    ya = (e1 * o1_ref[...].astype(F32) + e2 * o2_ref[...].astype(F32)
          + e3 * o3_ref[...].astype(F32)) / (e1 + e2 + e3)
    out_ref[...] = (x_ref[...] + _dot(ya.astype(BF16), wa_ref[...])
                    + _dot(yb_ref[...], wb_ref[...]))


def _ab_out(x2, outs, lses, yb, w_out):
    t = x2.shape[0]
    row = lambda w: pl.BlockSpec((ROW_TILE, w), lambda i: (i, 0))
    return pl.pallas_call(
        _ab_out_kernel,
        grid=(t // ROW_TILE,),
        in_specs=[row(D_MODEL)] + [row(A_WIDTH)] * 6 + [row(B_CHANNELS),
                  _const_spec((A_WIDTH, D_MODEL)), _const_spec((B_CHANNELS, D_MODEL))],
        out_specs=row(D_MODEL),
        out_shape=jax.ShapeDtypeStruct((t, D_MODEL), F32),
        compiler_params=_cparams(("parallel",)),
        name="ab_out_proj",
    )(x2, *outs, *lses, yb, w_out[:A_WIDTH].astype(BF16), w_out[A_WIDTH:].astype(BF16))


def _mem_kv_kernel(mem_ref, g_ref, wk_ref, wv_ref, k_ref, v_ref):
    n = _rms(mem_ref[...], g_ref[...]).astype(BF16)
    k_ref[...] = _dot(n, wk_ref[...]).astype(BF16)
    v_ref[...] = _dot(n, wv_ref[...]).astype(BF16)


def _mem_kv(mem2, g, wkv):
    rows = mem2.shape[0]
    w = X_HEADS * X_HEAD_DIM
    return pl.pallas_call(
        _mem_kv_kernel,
        grid=(1,),
        in_specs=[_const_spec((rows, D_MODEL)), _const_spec((1, D_MODEL)),
                  _const_spec((D_MODEL, w)), _const_spec((D_MODEL, w))],
        out_specs=[_const_spec((rows, w)), _const_spec((rows, w))],
        out_shape=[jax.ShapeDtypeStruct((rows, w), BF16)] * 2,
        compiler_params=_cparams(("arbitrary",)),
        name="mem_kv_proj",
    )(mem2, g.reshape(1, D_MODEL), wkv[:, :w].astype(BF16), wkv[:, w:].astype(BF16))


def _cross_kernel(x_ref, g_ref, wq_ref, k_ref, v_ref, wo_ref, out_ref):
    x = x_ref[0]
    n = _rms(x, g_ref[...]).astype(BF16)
    q = (_dot(n, wq_ref[...]) * (X_HEAD_DIM ** -0.5)).astype(BF16)
    heads = []
    for h in range(X_HEADS):
        cols = slice(h * X_HEAD_DIM, (h + 1) * X_HEAD_DIM)
        s = _dot_t(q[:, cols], k_ref[0, :, cols])
        m = jnp.max(s, axis=-1, keepdims=True)
        p = jnp.exp(s - m)
        l = jnp.sum(p, axis=-1, keepdims=True)
        heads.append((_dot(p.astype(BF16), v_ref[0, :, cols]) / l).astype(BF16))
    o = jnp.concatenate(heads, axis=-1)
    out_ref[0] = x + _dot(o, wo_ref[...])


def _cross_attention(x3, g, wq, k, v, wo):
    b, s, _ = x3.shape
    m = k.shape[1]
    w = X_HEADS * X_HEAD_DIM
    return pl.pallas_call(
        _cross_kernel,
        grid=(b, s // ROW_TILE),
        in_specs=[
            pl.BlockSpec((1, ROW_TILE, D_MODEL), lambda bi, i: (bi, i, 0)),
            _const_spec((1, D_MODEL)),
            _const_spec((D_MODEL, w)),
            pl.BlockSpec((1, m, w), lambda bi, i: (bi, 0, 0)),
            pl.BlockSpec((1, m, w), lambda bi, i: (bi, 0, 0)),
            _const_spec((w, D_MODEL)),
        ],
        out_specs=pl.BlockSpec((1, ROW_TILE, D_MODEL), lambda bi, i: (bi, i, 0)),
        out_shape=jax.ShapeDtypeStruct((b, s, D_MODEL), F32),
        compiler_params=_cparams(("parallel", "parallel")),
        name="cross_attn",
    )(x3, g.reshape(1, D_MODEL), wq.astype(BF16), k, v, wo.astype(BF16))


def _mlp_kernel(x_ref, g_ref, w1_ref, w2_ref, gf_ref, out_ref, *, final_norm):
    x = x_ref[...]
    n = _rms(x, g_ref[...]).astype(BF16)
    acc = x
    for c in range(D_FF // FF_CHUNK):
        cols = slice(c * FF_CHUNK, (c + 1) * FF_CHUNK)
        h = jnp.maximum(_dot(n, w1_ref[:, cols]), 0.0)
        acc = acc + _dot((h * h).astype(BF16), w2_ref[cols, :])
    if final_norm:
        acc = _rms(acc, gf_ref[...])
    out_ref[...] = acc


def _mlp(x2, g, w1, w2, gf, final_norm):
    t = x2.shape[0]
    return pl.pallas_call(
        functools.partial(_mlp_kernel, final_norm=final_norm),
        grid=(t // ROW_TILE,),
        in_specs=[
            pl.BlockSpec((ROW_TILE, D_MODEL), lambda i: (i, 0)),
            _const_spec((1, D_MODEL)),
            _const_spec((D_MODEL, D_FF)),
            _const_spec((D_FF, D_MODEL)),
            _const_spec((1, D_MODEL)),
        ],
        out_specs=pl.BlockSpec((ROW_TILE, D_MODEL), lambda i: (i, 0)),
        out_shape=jax.ShapeDtypeStruct((t, D_MODEL), F32),
        compiler_params=_cparams(("parallel",)),
        name="mlp_final" if final_norm else "mlp",
    )(x2, g.reshape(1, D_MODEL), w1.astype(BF16), w2.astype(BF16), gf.reshape(1, D_MODEL))


def _rope_tile(x, cos_t, sin_lo, sin_hi):
    half = D_ROPE_DIM // 2
    return (x * cos_t + pltpu.roll(x, D_PAD_DIM - half, 1) * sin_lo
            + pltpu.roll(x, half, 1) * sin_hi)


def _cd_in_kernel(x_ref, pos_ref, g_ref, freq_ref, wqkv_ref, wcq_ref, wckv_ref, wkr_ref,
                  qng_ref, kvng_ref, wuq_ref, wuk_ref, wuv_ref,
                  qct_ref, kc_ref, vct_ref, qdt_ref, kd_ref, vdt_ref):
    n = _rms(x_ref[0], g_ref[...]).astype(BF16)
    qkv = _dot(n, wqkv_ref[...])
    qct_ref[0, 0] = (qkv[:, :C_WIDTH] * (C_HEAD_DIM ** -0.5 * LOG2E)).T.astype(BF16)
    kc_ref[0] = qkv[:, C_WIDTH:2 * C_WIDTH].astype(BF16)
    vct_ref[0, 0] = qkv[:, 2 * C_WIDTH:].T.astype(BF16)

    ang = pos_ref[0].astype(F32) * freq_ref[...]
    cos_t, sin_t = jnp.cos(ang), jnp.sin(ang)
    lane = lax.broadcasted_iota(jnp.int32, ang.shape, 1)
    split = D_NOPE_DIM + D_ROPE_DIM // 2
    sin_lo = jnp.where(lane < split, -sin_t, 0.0)
    sin_hi = jnp.where(lane >= split, sin_t, 0.0)

    cq = _rms(_dot(n, wcq_ref[...]), qng_ref[...]).astype(BF16)
    q = _dot(cq, wuq_ref[...])
    ckv = _rms(_dot(n, wckv_ref[...]), kvng_ref[...]).astype(BF16)
    k_nope = _dot(ckv, wuk_ref[...])
    vdt_ref[0, 0] = _dot(ckv, wuv_ref[...]).T.astype(BF16)
    k_rope = _rope_tile(_dot(n, wkr_ref[...]), cos_t, sin_lo, sin_hi)
    scale = (D_NOPE_DIM + D_ROPE_DIM) ** -0.5 * LOG2E
    for h in range(D_HEADS):
        cols = slice(h * D_PAD_DIM, (h + 1) * D_PAD_DIM)
        qdt_ref[0, 0, cols, :] = (
            _rope_tile(q[:, cols], cos_t, sin_lo, sin_hi) * scale).T.astype(BF16)
        kd_ref[0, :, cols] = (k_nope[:, cols] + k_rope).astype(BF16)


def _pad_heads(w, heads, width):
    k = w.shape[0]
    w = w.reshape(k, heads, width)
    return jnp.pad(w, ((0, 0), (0, 0), (0, D_PAD_DIM - width))).reshape(k, heads * D_PAD_DIM)


def _cd_in(x3, positions, g, w_in, q_norm_g, kv_norm_g, w_uq, w_uk, w_uv):
    b, s, _ = x3.shape
    o3 = 3 * C_WIDTH
    o4 = o3 + D_Q_RANK
    o5 = o4 + D_KV_RANK
    wqkv = w_in[:, :o3].astype(BF16)
    wcq = w_in[:, o3:o4].astype(BF16)
    wckv = w_in[:, o4:o5].astype(BF16)
    wkr = jnp.pad(w_in[:, o5:], ((0, 0), (D_NOPE_DIM, D_PAD_DIM - D_NOPE_DIM - D_ROPE_DIM))).astype(BF16)
    wuq = _pad_heads(w_uq, D_HEADS, D_NOPE_DIM + D_ROPE_DIM).astype(BF16)
    wuk = _pad_heads(w_uk, D_HEADS, D_NOPE_DIM).astype(BF16)
    half = D_ROPE_DIM // 2
    inv_freq = ROPE_THETA ** (-jnp.arange(half, dtype=F32) / half)
    freq = jnp.zeros((1, D_PAD_DIM), F32).at[0, D_NOPE_DIM:D_NOPE_DIM + D_ROPE_DIM].set(
        jnp.concatenate([inv_freq, inv_freq]))
    dw = D_HEADS * D_PAD_DIM
    nt = s // ROW_TILE
    row = lambda w: pl.BlockSpec((1, ROW_TILE, w), lambda bi, i: (bi, i, 0))
    colmajor = lambda w: pl.BlockSpec((1, 1, w, ROW_TILE), lambda bi, i: (bi, i, 0, 0))
    return pl.pallas_call(
        _cd_in_kernel,
        grid=(b, s // ROW_TILE),
        in_specs=[
            row(D_MODEL), row(1), _const_spec((1, D_MODEL)), _const_spec((1, D_PAD_DIM)),
            _const_spec((D_MODEL, o3)), _const_spec((D_MODEL, D_Q_RANK)),
            _const_spec((D_MODEL, D_KV_RANK)), _const_spec((D_MODEL, D_PAD_DIM)),
            _const_spec((1, D_Q_RANK)), _const_spec((1, D_KV_RANK)),
            _const_spec((D_Q_RANK, dw)), _const_spec((D_KV_RANK, dw)),
            _const_spec((D_KV_RANK, D_HEADS * D_V_DIM)),
        ],
        out_specs=[colmajor(C_WIDTH), row(C_WIDTH), colmajor(C_WIDTH), colmajor(dw), row(dw),
                   colmajor(D_HEADS * D_V_DIM)],
        out_shape=[
            jax.ShapeDtypeStruct((b, nt, C_WIDTH, ROW_TILE), BF16),
            jax.ShapeDtypeStruct((b, s, C_WIDTH), BF16),
            jax.ShapeDtypeStruct((b, nt, C_WIDTH, ROW_TILE), BF16),
            jax.ShapeDtypeStruct((b, nt, dw, ROW_TILE), BF16),
            jax.ShapeDtypeStruct((b, s, dw), BF16),
            jax.ShapeDtypeStruct((b, nt, D_HEADS * D_V_DIM, ROW_TILE), BF16),
        ],
        compiler_params=_cparams(("parallel", "parallel")),
        name="cd_in_proj",
    )(x3, positions.reshape(b, s, 1), g.reshape(1, D_MODEL), freq, wqkv, wcq, wckv, wkr,
      q_norm_g.reshape(1, D_Q_RANK), kv_norm_g.reshape(1, D_KV_RANK), wuq, wuk,
      w_uv.astype(BF16))


def _causal_kernel(qt_ref, k_ref, vt_ref, *rest, mode, lam_init):
    if mode == "diff":
        lq1_ref, lk1_ref, lq2_ref, lk2_ref, g_ref, out_ref = rest[:6]
    else:
        out_ref = rest[0]
    s_bufs, mx_bufs = rest[-7:-5], rest[-5:-3]
    m_ref, l_ref, acc_ref = rest[-3:]
    t = ATTN_TILE
    qi = pl.program_id(2)
    qt = qt_ref[0, 0]
    feat = lax.broadcasted_iota(jnp.int32, (128, t), 0)
    if mode == "diff":
        zero = jnp.zeros_like(qt)
        qs = (jnp.where(feat < C_HEAD_DIM, qt, zero), jnp.where(feat >= C_HEAD_DIM, qt, zero))
    else:
        qs = (qt[:D_PAD_DIM], qt[D_PAD_DIM:])

    def keys(j, rows):
        if mode == "diff":
            return k_ref[0, rows, :]
        return k_ref[0, rows, j * D_PAD_DIM:(j + 1) * D_PAD_DIM]

    def scores(kb, buf):
        rows = pl.ds(pl.multiple_of(kb * t, t), t)
        for j in range(2):
            s = _dot(keys(j, rows), qs[j])
            s_bufs[buf][j] = s
            mx_bufs[buf][j] = jnp.max(s, axis=0, keepdims=True)

    def consume(kb, buf, masked):
        vt = vt_ref[0, kb]
        for j in range(2):
            s = s_bufs[buf][j]
            if masked:
                kr = lax.broadcasted_iota(jnp.int32, (t, t), 0)
                qc = lax.broadcasted_iota(jnp.int32, (t, t), 1)
                s = jnp.where(kr <= qc, s, NEG)
                block_max = jnp.max(s, axis=0, keepdims=True)
            else:
                block_max = mx_bufs[buf][j]
            m = m_ref[j]
            m_new = jnp.maximum(m, block_max)
            alpha = jnp.exp2(m - m_new)
            p = jnp.exp2(s - m_new)
            m_ref[j] = m_new
            l_ref[j] = alpha * l_ref[j] + jnp.sum(p, axis=0, keepdims=True)
            acc_ref[j] = alpha * acc_ref[j] + _dot(vt, p.astype(BF16))

    m_ref[...] = jnp.full(m_ref.shape, NEG, F32)
    l_ref[...] = jnp.zeros(l_ref.shape, F32)
    acc_ref[...] = jnp.zeros(acc_ref.shape, F32)
    scores(0, 0)

    def body(kb, carry):
        for parity in range(2):
            @pl.when(kb % 2 == parity)
            def _():
                scores(kb + 1, 1 - parity)
                consume(kb, parity, False)
        return carry

    lax.fori_loop(0, qi, body, 0)
    for parity in range(2):
        @pl.when(qi % 2 == parity)
        def _():
            consume(qi, parity, True)

    a1 = acc_ref[0] / l_ref[0]
    a2 = acc_ref[1] / l_ref[1]
    if mode == "diff":
        lam = (jnp.exp(jnp.sum(lq1_ref[...] * lk1_ref[...], axis=-1, keepdims=True))
               - jnp.exp(jnp.sum(lq2_ref[...] * lk2_ref[...], axis=-1, keepdims=True)) + lam_init)
        d = a1 - lam * a2
        out = (d * lax.rsqrt(jnp.mean(d * d, axis=0, keepdims=True) + EPS) * g_ref[...]
               * (1.0 - lam_init))
    else:
        out = jnp.where(feat < D_V_DIM, a1, a2)
    out_ref[0] = out.T.astype(BF16)


def _causal_attention(qt, k, vt, mode, extras=(), lam_init=0.0):
    b, s, _ = k.shape
    t = ATTN_TILE
    assert qt.shape[3] == t and vt.shape[3] == t
    qw = 128 if mode == "diff" else 2 * D_PAD_DIM
    groups = k.shape[2] // qw
    in_specs = [
        pl.BlockSpec((1, 1, qw, t), lambda bi, h, i: (bi, i, h, 0)),
        pl.BlockSpec((1, s, qw), lambda bi, h, i: (bi, 0, h)),
        pl.BlockSpec((1, s // t, 128, t), lambda bi, h, i: (bi, 0, h, 0)),
    ] + [_const_spec(e.shape) for e in extras]
    return pl.pallas_call(
        functools.partial(_causal_kernel, mode=mode, lam_init=lam_init),
        grid=(b, groups, s // t),
        in_specs=in_specs,
        out_specs=pl.BlockSpec((1, t, 128), lambda bi, h, i: (bi, i, h)),
        out_shape=jax.ShapeDtypeStruct((b, s, groups * 128), BF16),
        scratch_shapes=[
            pltpu.VMEM((2, t, t), F32), pltpu.VMEM((2, t, t), F32),
            pltpu.VMEM((2, 1, t), F32), pltpu.VMEM((2, 1, t), F32),
            pltpu.VMEM((2, 1, t), F32), pltpu.VMEM((2, 1, t), F32),
            pltpu.VMEM((2, 128, t), F32),
        ],
        compiler_params=_cparams(("parallel", "parallel", "arbitrary")),
        name=f"causal_attn_{mode}",
    )(qt, k, vt, *extras)


def _cd_out_kernel(x_ref, yc_ref, yd_ref, wc_ref, wd_ref, out_ref):
    out_ref[...] = x_ref[...] + _dot(yc_ref[...], wc_ref[...]) + _dot(yd_ref[...], wd_ref[...])


def _cd_out(x2, yc, yd, w_out):
    t = x2.shape[0]
    row = lambda w: pl.BlockSpec((ROW_TILE, w), lambda i: (i, 0))
    dvw = D_HEADS * D_V_DIM
    return pl.pallas_call(
        _cd_out_kernel,
        grid=(t // ROW_TILE,),
        in_specs=[row(D_MODEL), row(C_WIDTH), row(dvw),
                  _const_spec((C_WIDTH, D_MODEL)), _const_spec((dvw, D_MODEL))],
        out_specs=row(D_MODEL),
        out_shape=jax.ShapeDtypeStruct((t, D_MODEL), F32),
        compiler_params=_cparams(("parallel",)),
        name="cd_out_proj",
    )(x2, yc, yd, w_out[:C_WIDTH].astype(BF16), w_out[C_WIDTH:].astype(BF16))


def kernel(x, mem, positions, norm_mix_g, norm_cross_g, norm_mem_g, cross_wq, cross_wkv, cross_wo,
           norm_mlp_g, mlp_w1, mlp_w2, ab_w_in, ab_w_out, ab_conv_w, ab_conv_b, ab_ln_g, ab_ln_b,
           cd_w_in, cd_w_out, diff_lq1, diff_lk1, diff_lq2, diff_lk2, diff_subln_g, mla_q_norm_g,
           mla_kv_norm_g, mla_w_uq, mla_w_uk, mla_w_uv, final_norm_g):
    b, s, d = x.shape
    t = b * s
    mem_len = mem.shape[1]
    depth = norm_mix_g.shape[0]
    xw = X_HEADS * X_HEAD_DIM
    x2 = x.reshape(t, d)
    for i in range(depth):
        j = i // 2
        if i % 2 == 0:
            qkv, glu = _ab_in(x2, norm_mix_g[i], ab_w_in[j])
            qkv3 = qkv.reshape(b, s, 3 * A_WIDTH)
            outs, lses = zip(*[_dilated_pattern(qkv3, dil) for _, dil in A_PATTERNS])
            yb = _conv_module(glu.reshape(b, s, B_CHANNELS), ab_conv_w[j], ab_conv_b[j],
                              ab_ln_g[j], ab_ln_b[j])
            x2 = _ab_out(x2, outs, lses, yb.reshape(t, B_CHANNELS), ab_w_out[j])
        else:
            qc, kc, vc, qd, kd, vd = _cd_in(
                x2.reshape(b, s, d), positions, norm_mix_g[i], cd_w_in[j], mla_q_norm_g[j],
                mla_kv_norm_g[j], mla_w_uq[j], mla_w_uk[j], mla_w_uv[j])
            lam_init = 0.8 - 0.6 * math.exp(-0.3 * i)
            vec = lambda a: a.reshape(1, -1)
            yc = _causal_attention(
                qc, kc, vc, "diff",
                extras=(vec(diff_lq1[j]), vec(diff_lk1[j]), vec(diff_lq2[j]), vec(diff_lk2[j]),
                        diff_subln_g[j].reshape(-1, 1)),
                lam_init=lam_init)
            yd = _causal_attention(qd, kd, vd, "mla")
            x2 = _cd_out(x2, yc.reshape(t, C_WIDTH), yd.reshape(t, D_HEADS * D_V_DIM), cd_w_out[j])
        km, vm = _mem_kv(mem.reshape(b * mem_len, d), norm_mem_g[i], cross_wkv[i])
        x2 = _cross_attention(x2.reshape(b, s, d), norm_cross_g[i], cross_wq[i],
                              km.reshape(b, mem_len, xw), vm.reshape(b, mem_len, xw),
                              cross_wo[i]).reshape(t, d)
        x2 = _mlp(x2, norm_mlp_g[i], mlp_w1[i], mlp_w2[i], final_norm_g, i == depth - 1)
    return x2.reshape(b, s, d)
```

```python
import functools
import math

import jax
import jax.numpy as jnp
from jax import lax
from jax.experimental import pallas as pl
from jax.experimental.pallas import tpu as pltpu

F32 = jnp.float32
BF16 = jnp.bfloat16

D_MODEL = 1024
EPS = 1e-6
LOG2E = math.log2(math.e)
NEG = -1e30

A_HEADS = 8
A_HEAD_DIM = 64
A_WIDTH = A_HEADS * A_HEAD_DIM
A_PATTERNS = ((128, 1), (512, 4), (2048, 16))
A_BLK = 128
assert all(window // dilation == A_BLK for window, dilation in A_PATTERNS)
B_CHANNELS = 512
B_CONV_WIDTH = 31
B_HALO = 32
C_HEADS = 4
C_HEAD_DIM = 64
C_WIDTH = C_HEADS * 2 * C_HEAD_DIM
D_HEADS = 8
D_NOPE_DIM = 64
D_ROPE_DIM = 32
D_V_DIM = 64
D_Q_RANK = 384
D_KV_RANK = 256
D_PAD_DIM = 128
ROPE_THETA = 10000.0
X_HEADS = 4
X_HEAD_DIM = 128
D_FF = 4 * D_MODEL
FF_CHUNK = 1024

ROW_TILE = 512
ATTN_TILE = ROW_TILE
VMEM_LIMIT = 56 * 1024 * 1024


def _cparams(sem):
    return pltpu.CompilerParams(dimension_semantics=sem, vmem_limit_bytes=VMEM_LIMIT)


def _rms(x, g):
    return x * lax.rsqrt(jnp.mean(x * x, axis=-1, keepdims=True) + EPS) * g


def _dot(a, b):
    return jnp.dot(a, b, preferred_element_type=F32)


def _dot_t(a, b):
    return lax.dot_general(a, b, (((1,), (1,)), ((), ())), preferred_element_type=F32)


def _const_spec(shape):
    nd = len(shape)
    return pl.BlockSpec(shape, lambda *_: (0,) * nd)


def _ab_in_kernel(x_ref, g_ref, wqkv_ref, wu_ref, wg_ref, *rest):
    views, glu_ref, qkv_scr = rest[:-2], rest[-2], rest[-1]
    n = _rms(x_ref[0], g_ref[...]).astype(BF16)
    qkv = _dot(n, wqkv_ref[...])
    lane = lax.broadcasted_iota(jnp.int32, qkv.shape, 1)
    qkv = jnp.where(lane < A_WIDTH, qkv * (A_HEAD_DIM ** -0.5 * LOG2E), qkv)
    width = 3 * A_WIDTH
    for c in range(width // 128):
        qkv_scr[c] = qkv[:, c * 128:(c + 1) * 128]
    for (_, d), view_ref in zip(A_PATTERNS, views):
        if d == 1:
            view_ref[0] = qkv.astype(BF16)
            continue
        for r in range(d):
            for c in range(width // 128):
                view_ref[0, :, r * width + c * 128:r * width + (c + 1) * 128] = (
                    qkv_scr[c, pl.ds(r, ROW_TILE // d, stride=d), :].astype(BF16))
    u = _dot(n, wu_ref[...])
    gate = _dot(n, wg_ref[...])
    glu_ref[0] = u * jax.nn.sigmoid(gate)


def _ab_in(x3, g, w_in):
    b, s, _ = x3.shape
    width = 3 * A_WIDTH
    wqkv = w_in[:, :width].astype(BF16)
    wu = w_in[:, width:width + B_CHANNELS].astype(BF16)
    wg = w_in[:, width + B_CHANNELS:].astype(BF16)
    row = lambda rows, w: pl.BlockSpec((1, rows, w), lambda bi, i: (bi, i, 0))
    dils = [d for _, d in A_PATTERNS]
    return pl.pallas_call(
        _ab_in_kernel,
        grid=(b, s // ROW_TILE),
        in_specs=[
            row(ROW_TILE, D_MODEL),
            _const_spec((1, D_MODEL)),
            _const_spec((D_MODEL, width)),
            _const_spec((D_MODEL, B_CHANNELS)),
            _const_spec((D_MODEL, B_CHANNELS)),
        ],
        out_specs=[row(ROW_TILE // d, d * width) for d in dils] + [row(ROW_TILE, B_CHANNELS)],
        out_shape=[jax.ShapeDtypeStruct((b, s // d, d * width), BF16) for d in dils]
        + [jax.ShapeDtypeStruct((b, s, B_CHANNELS), F32)],
        scratch_shapes=[pltpu.VMEM((width // 128, ROW_TILE, 128), F32)],
        compiler_params=_cparams(("parallel", "parallel")),
        name="ab_in_proj",
    )(x3, g.reshape(1, D_MODEL), wqkv, wu, wg)


def _dilated_kernel(q_ref, kp_ref, kc_ref, vp_ref, vc_ref, o_ref, lse_ref, s_ref, *, rows):
    first_tile = pl.program_id(2) == 0
    qi = lax.broadcasted_iota(jnp.int32, (A_BLK, 2 * A_BLK), 0)
    kj = lax.broadcasted_iota(jnp.int32, (A_BLK, 2 * A_BLK), 1)
    band = ((kj < A_BLK) & (kj >= qi)) | ((kj >= A_BLK) & (kj - A_BLK <= qi))
    lane = lax.broadcasted_iota(jnp.int32, (A_BLK, 128), 1)
    low = lane < A_HEAD_DIM
    pairs = A_WIDTH // 128
    nblk = rows // A_BLK

    def blk(i):
        return slice(i * A_BLK, (i + 1) * A_BLK)

    def prev_and_cur(i, prev_ref, cur_ref, cols):
        prev = prev_ref[0, :, cols] if i == 0 else cur_ref[0, blk(i - 1), cols]
        return jnp.concatenate([prev, cur_ref[0, blk(i), cols]], axis=0)

    def scores(i):
        for pr in range(pairs):
            cols = slice(pr * 128, (pr + 1) * 128)
            q = q_ref[0, blk(i), cols]
            k = prev_and_cur(i, kp_ref, kc_ref, cols)
            zero = jnp.zeros_like(q)
            s_ref[i % 2, 2 * pr] = _dot_t(jnp.where(low, q, zero), k)
            s_ref[i % 2, 2 * pr + 1] = _dot_t(jnp.where(low, zero, q), k)

    def consume(i):
        valid = band & ((kj >= A_BLK) | jnp.logical_not(first_tile)) if i == 0 else band
        for pr in range(pairs):
            cols = slice(pr * 128, (pr + 1) * 128)
            v = prev_and_cur(i, vp_ref, vc_ref, cols)
            outs, lses = [], []
            for half in range(2):
                s = jnp.where(valid, s_ref[i % 2, 2 * pr + half], NEG)
                m = jnp.max(s, axis=-1, keepdims=True)
                p = jnp.exp2(s - m)
                l = jnp.sum(p, axis=-1, keepdims=True)
                outs.append(_dot(p.astype(BF16), v) / l)
                lses.append(m + jnp.log2(l))
            o_ref[0, blk(i), cols] = jnp.where(low, outs[0], outs[1]).astype(BF16)
            lse_ref[0, blk(i), cols] = jnp.where(low, lses[0], lses[1])

    scores(0)
    for i in range(nblk):
        if i + 1 < nblk:
            scores(i + 1)
        consume(i)


def _dilated_pattern(view, dilation):
    b, sub, _ = view.shape
    rows = min(sub, 512)
    assert sub % rows == 0 and rows % A_BLK == 0
    blocks_per_tile = rows // A_BLK

    def cur_spec(part):
        return pl.BlockSpec((1, rows, A_WIDTH), lambda bi, r, n: (bi, n, 3 * r + part))

    def prev_spec(part):
        return pl.BlockSpec(
            (1, A_BLK, A_WIDTH),
            lambda bi, r, n: (bi, jnp.maximum(n * blocks_per_tile - 1, 0), 3 * r + part))

    out_spec = pl.BlockSpec((1, rows, A_WIDTH), lambda bi, r, n: (bi, n, r))
    return pl.pallas_call(
        functools.partial(_dilated_kernel, rows=rows),
        grid=(b, dilation, sub // rows),
        in_specs=[cur_spec(0), prev_spec(1), cur_spec(1), prev_spec(2), cur_spec(2)],
        out_specs=[out_spec, out_spec],
        out_shape=[
            jax.ShapeDtypeStruct((b, sub, dilation * A_WIDTH), BF16),
            jax.ShapeDtypeStruct((b, sub, dilation * A_WIDTH), F32),
        ],
        scratch_shapes=[pltpu.VMEM((2, A_HEADS, A_BLK, 2 * A_BLK), F32)],
        compiler_params=_cparams(("parallel", "parallel", "arbitrary")),
        name=f"dilated_attn_d{dilation}",
    )(view, view, view, view, view)


CONV_ROWS = 32
SUBLANES = 8


def _conv_kernel(halo_ref, cur_ref, w_ref, cb_ref, g_ref, b_ref, out_ref, buf_ref, *, rows):
    first_tile = pl.program_id(1) == 0
    halo = halo_ref[0]
    buf_ref[0:B_HALO, :] = jnp.where(first_tile, jnp.zeros_like(halo), halo)
    buf_ref[B_HALO:B_HALO + rows, :] = cur_ref[0]
    lead = B_HALO - (B_CONV_WIDTH - 1)

    def chunk(c, carry):
        base = pl.multiple_of(c * CONV_ROWS, CONV_ROWS)
        span = CONV_ROWS + B_HALO
        accs = []
        for c in range(B_CHANNELS // 128):
            cols = slice(c * 128, (c + 1) * 128)
            win = buf_ref[pl.ds(base, span), cols]
            acc = jnp.zeros((CONV_ROWS, 128), F32)
            for r in range(SUBLANES):
                shifted = pltpu.roll(win, span - r, 0) if r else win
                for tap in range(B_CONV_WIDTH):
                    if (lead + tap) % SUBLANES == r:
                        a = (lead + tap) // SUBLANES * SUBLANES
                        acc = acc + shifted[a:a + CONV_ROWS, :] * w_ref[tap:tap + 1, cols]
            accs.append(acc)
        y = jnp.concatenate(accs, axis=-1) + cb_ref[...]
        mu = jnp.mean(y, axis=-1, keepdims=True)
        yc = y - mu
        var = jnp.mean(yc * yc, axis=-1, keepdims=True)
        z = yc * lax.rsqrt(var + EPS) * g_ref[...] + b_ref[...]
        out_ref[0, pl.ds(base, CONV_ROWS), :] = (z * jax.nn.sigmoid(z)).astype(BF16)
        return carry

    lax.fori_loop(0, rows // CONV_ROWS, chunk, 0)


def _conv_module(glu, conv_w, conv_b, ln_g, ln_b):
    b, s, c = glu.shape
    rows = ROW_TILE
    halo_blocks = rows // B_HALO
    vec = lambda a: a.reshape(1, c)
    return pl.pallas_call(
        functools.partial(_conv_kernel, rows=rows),
        grid=(b, s // rows),
        in_specs=[
            pl.BlockSpec((1, B_HALO, c), lambda bi, n: (bi, jnp.maximum(n * halo_blocks - 1, 0), 0)),
            pl.BlockSpec((1, rows, c), lambda bi, n: (bi, n, 0)),
            _const_spec((B_CONV_WIDTH, c)),
            _const_spec((1, c)), _const_spec((1, c)), _const_spec((1, c)),
        ],
        out_specs=pl.BlockSpec((1, rows, c), lambda bi, n: (bi, n, 0)),
        out_shape=jax.ShapeDtypeStruct((b, s, c), BF16),
        scratch_shapes=[pltpu.VMEM((B_HALO + rows, c), F32)],
        compiler_params=_cparams(("parallel", "arbitrary")),
        name="conformer_conv",
    )(glu, glu, conv_w.reshape(B_CONV_WIDTH, c), vec(conv_b), vec(ln_g), vec(ln_b))


def _ab_out_kernel(x_ref, *rest):
    n_pat = len(A_PATTERNS)
    o_refs, l_refs = rest[:n_pat], rest[n_pat:2 * n_pat]
    yb_ref, wa_ref, wb_ref, out_ref = rest[2 * n_pat:2 * n_pat + 4]
    scr = list(rest[2 * n_pat + 4:])
    tiles = A_WIDTH // 128
    gather = lambda ref: jnp.concatenate([ref[c] for c in range(tiles)], axis=-1)
    outs, lses = [], []
    for idx, (_, d) in enumerate(A_PATTERNS):
        if d == 1:
            outs.append(o_refs[idx][0].astype(F32))
            lses.append(l_refs[idx][0])
            continue
        o_scr, l_scr = scr.pop(0), scr.pop(0)
        for r in range(d):
            dst = pl.ds(r, ROW_TILE // d, stride=d)
            for c in range(tiles):
                cols = slice(r * A_WIDTH + c * 128, r * A_WIDTH + (c + 1) * 128)
                o_scr[c, dst, :] = o_refs[idx][0, :, cols].astype(F32)
                l_scr[c, dst, :] = l_refs[idx][0, :, cols]
        outs.append(gather(o_scr))
        lses.append(gather(l_scr))
    m = functools.reduce(jnp.maximum, lses)
    es = [jnp.exp2(l - m) for l in lses]
    ya = sum(e * o for e, o in zip(es, outs)) / sum(es)
    out_ref[0] = (x_ref[0] + _dot(ya.astype(BF16), wa_ref[...])
                  + _dot(yb_ref[0], wb_ref[...]))


def _ab_out(x3, outs, lses, yb, w_out):
    b, s, _ = x3.shape
    row = lambda rows, w: pl.BlockSpec((1, rows, w), lambda bi, i: (bi, i, 0))
    views = [row(ROW_TILE // d, d * A_WIDTH) for _, d in A_PATTERNS]
    scratch = []
    for _, d in A_PATTERNS:
        if d > 1:
            scratch += [pltpu.VMEM((A_WIDTH // 128, ROW_TILE, 128), F32)] * 2
    return pl.pallas_call(
        _ab_out_kernel,
        grid=(b, s // ROW_TILE),
        in_specs=[row(ROW_TILE, D_MODEL)] + views + views + [
            row(ROW_TILE, B_CHANNELS),
            _const_spec((A_WIDTH, D_MODEL)), _const_spec((B_CHANNELS, D_MODEL))],
        out_specs=row(ROW_TILE, D_MODEL),
        out_shape=jax.ShapeDtypeStruct((b, s, D_MODEL), F32),
        scratch_shapes=scratch,
        compiler_params=_cparams(("parallel", "parallel")),
        name="ab_out_proj",
    )(x3, *outs, *lses, yb, w_out[:A_WIDTH].astype(BF16), w_out[A_WIDTH:].astype(BF16))


def _mem_kv_kernel(mem_ref, g_ref, wk_ref, wv_ref, k_ref, v_ref):
    n = _rms(mem_ref[...], g_ref[...]).astype(BF16)
    k_ref[...] = _dot(n, wk_ref[...]).astype(BF16)
    v_ref[...] = _dot(n, wv_ref[...]).astype(BF16)


def _mem_kv(mem2, g, wkv):
    rows = mem2.shape[0]
    w = X_HEADS * X_HEAD_DIM
    return pl.pallas_call(
        _mem_kv_kernel,
        grid=(1,),
        in_specs=[_const_spec((rows, D_MODEL)), _const_spec((1, D_MODEL)),
                  _const_spec((D_MODEL, w)), _const_spec((D_MODEL, w))],
        out_specs=[_const_spec((rows, w)), _const_spec((rows, w))],
        out_shape=[jax.ShapeDtypeStruct((rows, w), BF16)] * 2,
        compiler_params=_cparams(("arbitrary",)),
        name="mem_kv_proj",
    )(mem2, g.reshape(1, D_MODEL), wkv[:, :w].astype(BF16), wkv[:, w:].astype(BF16))


def _cross_kernel(x_ref, g_ref, wq_ref, k_ref, v_ref, wo_ref, out_ref):
    x = x_ref[0]
    n = _rms(x, g_ref[...]).astype(BF16)
    q = (_dot(n, wq_ref[...]) * (X_HEAD_DIM ** -0.5)).astype(BF16)
    heads = []
    for h in range(X_HEADS):
        cols = slice(h * X_HEAD_DIM, (h + 1) * X_HEAD_DIM)
        s = _dot_t(q[:, cols], k_ref[0, :, cols])
        m = jnp.max(s, axis=-1, keepdims=True)
        p = jnp.exp(s - m)
        l = jnp.sum(p, axis=-1, keepdims=True)
        heads.append((_dot(p.astype(BF16), v_ref[0, :, cols]) / l).astype(BF16))
    o = jnp.concatenate(heads, axis=-1)
    out_ref[0] = x + _dot(o, wo_ref[...])


def _cross_attention(x3, g, wq, k, v, wo):
    b, s, _ = x3.shape
    m = k.shape[1]
    w = X_HEADS * X_HEAD_DIM
    return pl.pallas_call(
        _cross_kernel,
        grid=(b, s // ROW_TILE),
        in_specs=[
            pl.BlockSpec((1, ROW_TILE, D_MODEL), lambda bi, i: (bi, i, 0)),
            _const_spec((1, D_MODEL)),
            _const_spec((D_MODEL, w)),
            pl.BlockSpec((1, m, w), lambda bi, i: (bi, 0, 0)),
            pl.BlockSpec((1, m, w), lambda bi, i: (bi, 0, 0)),
            _const_spec((w, D_MODEL)),
        ],
        out_specs=pl.BlockSpec((1, ROW_TILE, D_MODEL), lambda bi, i: (bi, i, 0)),
        out_shape=jax.ShapeDtypeStruct((b, s, D_MODEL), F32),
        compiler_params=_cparams(("parallel", "parallel")),
        name="cross_attn",
    )(x3, g.reshape(1, D_MODEL), wq.astype(BF16), k, v, wo.astype(BF16))


def _mlp_kernel(x_ref, g_ref, w1_ref, w2_ref, gf_ref, out_ref, *, final_norm):
    x = x_ref[...]
    n = _rms(x, g_ref[...]).astype(BF16)
    acc = x
    for c in range(D_FF // FF_CHUNK):
        cols = slice(c * FF_CHUNK, (c + 1) * FF_CHUNK)
        h = jnp.maximum(_dot(n, w1_ref[:, cols]), 0.0)
        acc = acc + _dot((h * h).astype(BF16), w2_ref[cols, :])
    if final_norm:
        acc = _rms(acc, gf_ref[...])
    out_ref[...] = acc


def _mlp(x2, g, w1, w2, gf, final_norm):
    t = x2.shape[0]
    return pl.pallas_call(
        functools.partial(_mlp_kernel, final_norm=final_norm),
        grid=(t // ROW_TILE,),
        in_specs=[
            pl.BlockSpec((ROW_TILE, D_MODEL), lambda i: (i, 0)),
            _const_spec((1, D_MODEL)),
            _const_spec((D_MODEL, D_FF)),
            _const_spec((D_FF, D_MODEL)),
            _const_spec((1, D_MODEL)),
        ],
        out_specs=pl.BlockSpec((ROW_TILE, D_MODEL), lambda i: (i, 0)),
        out_shape=jax.ShapeDtypeStruct((t, D_MODEL), F32),
        compiler_params=_cparams(("parallel",)),
        name="mlp_final" if final_norm else "mlp",
    )(x2, g.reshape(1, D_MODEL), w1.astype(BF16), w2.astype(BF16), gf.reshape(1, D_MODEL))


def _rope_tile(x, cos_t, sin_lo, sin_hi):
    half = D_ROPE_DIM // 2
    return (x * cos_t + pltpu.roll(x, D_PAD_DIM - half, 1) * sin_lo
            + pltpu.roll(x, half, 1) * sin_hi)


def _cd_in_kernel(x_ref, pos_ref, g_ref, freq_ref, wqkv_ref, wcq_ref, wckv_ref, wkr_ref,
                  qng_ref, kvng_ref, wuq_ref, wuk_ref, wuv_ref,
                  qct_ref, kc_ref, vct_ref, qdt_ref, kd_ref, vdt_ref):
    n = _rms(x_ref[0], g_ref[...]).astype(BF16)
    qkv = _dot(n, wqkv_ref[...])
    qct_ref[0, 0] = (qkv[:, :C_WIDTH] * (C_HEAD_DIM ** -0.5 * LOG2E)).T.astype(BF16)
    kc_ref[0] = qkv[:, C_WIDTH:2 * C_WIDTH].astype(BF16)
    vct_ref[0, 0] = qkv[:, 2 * C_WIDTH:].T.astype(BF16)

    ang = pos_ref[0].astype(F32) * freq_ref[...]
    cos_t, sin_t = jnp.cos(ang), jnp.sin(ang)
    lane = lax.broadcasted_iota(jnp.int32, ang.shape, 1)
    split = D_NOPE_DIM + D_ROPE_DIM // 2
    sin_lo = jnp.where(lane < split, -sin_t, 0.0)
    sin_hi = jnp.where(lane >= split, sin_t, 0.0)

    cq = _rms(_dot(n, wcq_ref[...]), qng_ref[...]).astype(BF16)
    q = _dot(cq, wuq_ref[...])
    ckv = _rms(_dot(n, wckv_ref[...]), kvng_ref[...]).astype(BF16)
    k_nope = _dot(ckv, wuk_ref[...])
    vdt_ref[0, 0] = _dot(ckv, wuv_ref[...]).T.astype(BF16)
    k_rope = _rope_tile(_dot(n, wkr_ref[...]), cos_t, sin_lo, sin_hi)
    scale = (D_NOPE_DIM + D_ROPE_DIM) ** -0.5 * LOG2E
    for h in range(D_HEADS):
        cols = slice(h * D_PAD_DIM, (h + 1) * D_PAD_DIM)
        qdt_ref[0, 0, cols, :] = (
            _rope_tile(q[:, cols], cos_t, sin_lo, sin_hi) * scale).T.astype(BF16)
        kd_ref[0, :, cols] = (k_nope[:, cols] + k_rope).astype(BF16)


def _pad_heads(w, heads, width):
    k = w.shape[0]
    w = w.reshape(k, heads, width)
    return jnp.pad(w, ((0, 0), (0, 0), (0, D_PAD_DIM - width))).reshape(k, heads * D_PAD_DIM)


def _cd_in(x3, positions, g, w_in, q_norm_g, kv_norm_g, w_uq, w_uk, w_uv):
    b, s, _ = x3.shape
    o3 = 3 * C_WIDTH
    o4 = o3 + D_Q_RANK
    o5 = o4 + D_KV_RANK
    wqkv = w_in[:, :o3].astype(BF16)
    wcq = w_in[:, o3:o4].astype(BF16)
    wckv = w_in[:, o4:o5].astype(BF16)
    wkr = jnp.pad(w_in[:, o5:], ((0, 0), (D_NOPE_DIM, D_PAD_DIM - D_NOPE_DIM - D_ROPE_DIM))).astype(BF16)
    wuq = _pad_heads(w_uq, D_HEADS, D_NOPE_DIM + D_ROPE_DIM).astype(BF16)
    wuk = _pad_heads(w_uk, D_HEADS, D_NOPE_DIM).astype(BF16)
    half = D_ROPE_DIM // 2
    inv_freq = ROPE_THETA ** (-jnp.arange(half, dtype=F32) / half)
    freq = jnp.zeros((1, D_PAD_DIM), F32).at[0, D_NOPE_DIM:D_NOPE_DIM + D_ROPE_DIM].set(
        jnp.concatenate([inv_freq, inv_freq]))
    dw = D_HEADS * D_PAD_DIM
    nt = s // ROW_TILE
    row = lambda w: pl.BlockSpec((1, ROW_TILE, w), lambda bi, i: (bi, i, 0))
    colmajor = lambda w: pl.BlockSpec((1, 1, w, ROW_TILE), lambda bi, i: (bi, i, 0, 0))
    return pl.pallas_call(
        _cd_in_kernel,
        grid=(b, s // ROW_TILE),
        in_specs=[
            row(D_MODEL), row(1), _const_spec((1, D_MODEL)), _const_spec((1, D_PAD_DIM)),
            _const_spec((D_MODEL, o3)), _const_spec((D_MODEL, D_Q_RANK)),
            _const_spec((D_MODEL, D_KV_RANK)), _const_spec((D_MODEL, D_PAD_DIM)),
            _const_spec((1, D_Q_RANK)), _const_spec((1, D_KV_RANK)),
            _const_spec((D_Q_RANK, dw)), _const_spec((D_KV_RANK, dw)),
            _const_spec((D_KV_RANK, D_HEADS * D_V_DIM)),
        ],
        out_specs=[colmajor(C_WIDTH), row(C_WIDTH), colmajor(C_WIDTH), colmajor(dw), row(dw),
                   colmajor(D_HEADS * D_V_DIM)],
        out_shape=[
            jax.ShapeDtypeStruct((b, nt, C_WIDTH, ROW_TILE), BF16),
            jax.ShapeDtypeStruct((b, s, C_WIDTH), BF16),
            jax.ShapeDtypeStruct((b, nt, C_WIDTH, ROW_TILE), BF16),
            jax.ShapeDtypeStruct((b, nt, dw, ROW_TILE), BF16),
            jax.ShapeDtypeStruct((b, s, dw), BF16),
            jax.ShapeDtypeStruct((b, nt, D_HEADS * D_V_DIM, ROW_TILE), BF16),
        ],
        compiler_params=_cparams(("parallel", "parallel")),
        name="cd_in_proj",
    )(x3, positions.reshape(b, s, 1), g.reshape(1, D_MODEL), freq, wqkv, wcq, wckv, wkr,
      q_norm_g.reshape(1, D_Q_RANK), kv_norm_g.reshape(1, D_KV_RANK), wuq, wuk,
      w_uv.astype(BF16))


def _causal_kernel(qt_ref, k_ref, vt_ref, *rest, mode, lam_init, n_tiles):
    if mode == "diff":
        lq1_ref, lk1_ref, lq2_ref, lk2_ref, g_ref, out_ref = rest[:6]
    else:
        out_ref = rest[0]
    s_bufs, mx_bufs = rest[-11:-7], rest[-7:-3]
    m_ref, l_ref, acc_ref = rest[-3:]
    t = ATTN_TILE
    n_pairs = n_tiles * (n_tiles + 1) // 2
    assert n_tiles >= 2 and n_pairs % 2 == 0
    feat = lax.broadcasted_iota(jnp.int32, (128, t), 0)

    def score(pair, buf, j):
        qi, kb = pair
        rows = pl.ds(pl.multiple_of(kb * t, t), t)
        if mode == "diff":
            qt = qt_ref[0, qi]
            mine = (feat < C_HEAD_DIM) if j == 0 else (feat >= C_HEAD_DIM)
            q, k = jnp.where(mine, qt, jnp.zeros_like(qt)), k_ref[0, rows, :]
        else:
            cols = slice(j * D_PAD_DIM, (j + 1) * D_PAD_DIM)
            q, k = qt_ref[0, qi, cols, :], k_ref[0, rows, cols]
        s = _dot(k, q)
        s_bufs[buf][j] = s
        mx_bufs[buf][j] = jnp.max(s, axis=0, keepdims=True)

    def finish_tile(qi):
        a1 = acc_ref[0] / l_ref[0]
        a2 = acc_ref[1] / l_ref[1]
        if mode == "diff":
            lam = (jnp.exp(jnp.sum(lq1_ref[...] * lk1_ref[...], axis=-1, keepdims=True))
                   - jnp.exp(jnp.sum(lq2_ref[...] * lk2_ref[...], axis=-1, keepdims=True))
                   + lam_init)
            d = a1 - lam * a2
            out = (d * lax.rsqrt(jnp.mean(d * d, axis=0, keepdims=True) + EPS) * g_ref[...]
                   * (1.0 - lam_init))
        else:
            out = jnp.where(feat < D_V_DIM, a1, a2)
        out_ref[0, pl.ds(pl.multiple_of(qi * t, t), t), :] = out.T.astype(BF16)

    def consume(pair, buf, j, diagonal):
        qi, kb = pair
        first = kb == 0
        s = s_bufs[buf][j]
        if diagonal:
            kr = lax.broadcasted_iota(jnp.int32, (t, t), 0)
            qc = lax.broadcasted_iota(jnp.int32, (t, t), 1)
            s = jnp.where(kr <= qc, s, NEG)
            block_max = jnp.max(s, axis=0, keepdims=True)
        else:
            block_max = mx_bufs[buf][j]
        m = jnp.where(first, NEG, m_ref[j])
        l = jnp.where(first, 0.0, l_ref[j])
        m_new = jnp.maximum(m, block_max)
        alpha = jnp.exp2(m - m_new)
        p = jnp.exp2(s - m_new)
        m_ref[j] = m_new
        l_ref[j] = alpha * l + jnp.sum(p, axis=0, keepdims=True)
        acc_ref[j] = alpha * acc_ref[j] + _dot(vt_ref[0, kb], p.astype(BF16))
        if diagonal and j == 1:
            finish_tile(qi)

    def advance(pair):
        qi, kb = pair
        end = kb == qi
        nqi, nkb = jnp.where(end, qi + 1, qi), jnp.where(end, 0, kb + 1)
        done = nqi >= n_tiles
        return jnp.where(done, qi, nqi), jnp.where(done, kb, nkb)

    acc_ref[...] = jnp.zeros(acc_ref.shape, F32)
    pair0 = (jnp.int32(0), jnp.int32(0))
    pair1 = advance(pair0)
    for j in range(2):
        score(pair0, 0, j)
        score(pair1, 1, j)

    def body(trip, carry):
        pa, pb = carry[:2], carry[2:]
        pc = advance(pb)
        pd = advance(pc)
        diag_a, diag_b = pa[0] == pa[1], pb[0] == pb[1]
        for half in range(2):
            ra, rb, wc, wd = 2 * half, 2 * half + 1, 2 - 2 * half, 3 - 2 * half
            for da, db, cond in ((False, False, jnp.logical_not(diag_a | diag_b)),
                                 (True, False, diag_a), (False, True, diag_b)):
                @pl.when(cond & (trip % 2 == half))
                def _():
                    score(pc, wc, 0)
                    score(pc, wc, 1)
                    consume(pa, ra, 0, da)
                    score(pd, wd, 0)
                    consume(pa, ra, 1, da)
                    consume(pb, rb, 0, db)
                    score(pd, wd, 1)
                    consume(pb, rb, 1, db)
        return (*pc, *pd)

    lax.fori_loop(0, n_pairs // 2, body, (*pair0, *pair1))


def _causal_attention(qt, k, vt, mode, extras=(), lam_init=0.0):
    b, s, _ = k.shape
    t = ATTN_TILE
    assert qt.shape[3] == t and vt.shape[3] == t
    qw = 128 if mode == "diff" else 2 * D_PAD_DIM
    groups = k.shape[2] // qw
    in_specs = [
        pl.BlockSpec((1, s // t, qw, t), lambda bi, h: (bi, 0, h, 0)),
        pl.BlockSpec((1, s, qw), lambda bi, h: (bi, 0, h)),
        pl.BlockSpec((1, s // t, 128, t), lambda bi, h: (bi, 0, h, 0)),
    ] + [_const_spec(e.shape) for e in extras]
    stats = [pltpu.VMEM((2, 1, t), F32)] * 2
    return pl.pallas_call(
        functools.partial(_causal_kernel, mode=mode, lam_init=lam_init, n_tiles=s // t),
        grid=(b, groups),
        in_specs=in_specs,
        out_specs=pl.BlockSpec((1, s, 128), lambda bi, h: (bi, 0, h)),
        out_shape=jax.ShapeDtypeStruct((b, s, groups * 128), BF16),
        scratch_shapes=[
            *[pltpu.VMEM((2, t, t), F32)] * 4,
            *stats, *stats,
            *stats,
            pltpu.VMEM((2, 128, t), F32),
        ],
        compiler_params=_cparams(("parallel", "parallel")),
        name=f"causal_attn_{mode}",
    )(qt, k, vt, *extras)


def _cd_out_kernel(x_ref, yc_ref, yd_ref, wc_ref, wd_ref, out_ref):
    out_ref[...] = x_ref[...] + _dot(yc_ref[...], wc_ref[...]) + _dot(yd_ref[...], wd_ref[...])


def _cd_out(x2, yc, yd, w_out):
    t = x2.shape[0]
    row = lambda w: pl.BlockSpec((ROW_TILE, w), lambda i: (i, 0))
    dvw = D_HEADS * D_V_DIM
    return pl.pallas_call(
        _cd_out_kernel,
        grid=(t // ROW_TILE,),
        in_specs=[row(D_MODEL), row(C_WIDTH), row(dvw),
                  _const_spec((C_WIDTH, D_MODEL)), _const_spec((dvw, D_MODEL))],
        out_specs=row(D_MODEL),
        out_shape=jax.ShapeDtypeStruct((t, D_MODEL), F32),
        compiler_params=_cparams(("parallel",)),
        name="cd_out_proj",
    )(x2, yc, yd, w_out[:C_WIDTH].astype(BF16), w_out[C_WIDTH:].astype(BF16))


def kernel(x, mem, positions, norm_mix_g, norm_cross_g, norm_mem_g, cross_wq, cross_wkv, cross_wo,
           norm_mlp_g, mlp_w1, mlp_w2, ab_w_in, ab_w_out, ab_conv_w, ab_conv_b, ab_ln_g, ab_ln_b,
           cd_w_in, cd_w_out, diff_lq1, diff_lk1, diff_lq2, diff_lk2, diff_subln_g, mla_q_norm_g,
           mla_kv_norm_g, mla_w_uq, mla_w_uk, mla_w_uv, final_norm_g):
    b, s, d = x.shape
    t = b * s
    mem_len = mem.shape[1]
    depth = norm_mix_g.shape[0]
    xw = X_HEADS * X_HEAD_DIM
    x2 = x.reshape(t, d)
    for i in range(depth):
        j = i // 2
        if i % 2 == 0:
            *views, glu = _ab_in(x2.reshape(b, s, d), norm_mix_g[i], ab_w_in[j])
            outs, lses = zip(*[_dilated_pattern(view, dil)
                               for view, (_, dil) in zip(views, A_PATTERNS)])
            yb = _conv_module(glu, ab_conv_w[j], ab_conv_b[j], ab_ln_g[j], ab_ln_b[j])
            x2 = _ab_out(x2.reshape(b, s, d), outs, lses, yb, ab_w_out[j]).reshape(t, d)
        else:
            qc, kc, vc, qd, kd, vd = _cd_in(
                x2.reshape(b, s, d), positions, norm_mix_g[i], cd_w_in[j], mla_q_norm_g[j],
                mla_kv_norm_g[j], mla_w_uq[j], mla_w_uk[j], mla_w_uv[j])
            lam_init = 0.8 - 0.6 * math.exp(-0.3 * i)
            vec = lambda a: a.reshape(1, -1)
            yc = _causal_attention(
                qc, kc, vc, "diff",
                extras=(vec(diff_lq1[j]), vec(diff_lk1[j]), vec(diff_lq2[j]), vec(diff_lk2[j]),
                        diff_subln_g[j].reshape(-1, 1)),
                lam_init=lam_init)
            yd = _causal_attention(qd, kd, vd, "mla")
            x2 = _cd_out(x2, yc.reshape(t, C_WIDTH), yd.reshape(t, D_HEADS * D_V_DIM), cd_w_out[j])
        km, vm = _mem_kv(mem.reshape(b * mem_len, d), norm_mem_g[i], cross_wkv[i])
        x2 = _cross_attention(x2.reshape(b, s, d), norm_cross_g[i], cross_wq[i],
                              km.reshape(b, mem_len, xw), vm.reshape(b, mem_len, xw),
                              cross_wo[i]).reshape(t, d)
        x2 = _mlp(x2, norm_mlp_g[i], mlp_w1[i], mlp_w2[i], final_norm_g, i == depth - 1)
    return x2.reshape(b, s, d)
```

```python
import functools
import math

import jax
import jax.numpy as jnp
from jax import lax
from jax.experimental import pallas as pl
from jax.experimental.pallas import tpu as pltpu

F32 = jnp.float32
BF16 = jnp.bfloat16

D_MODEL = 1024
EPS = 1e-6
LOG2E = math.log2(math.e)
NEG = -1e30

A_HEADS = 8
A_HEAD_DIM = 64
A_WIDTH = A_HEADS * A_HEAD_DIM
A_PATTERNS = ((128, 1), (512, 4), (2048, 16))
A_BLK = 128
assert all(window // dilation == A_BLK for window, dilation in A_PATTERNS)
B_CHANNELS = 512
B_CONV_WIDTH = 31
B_HALO = 32
C_HEADS = 4
C_HEAD_DIM = 64
C_WIDTH = C_HEADS * 2 * C_HEAD_DIM
D_HEADS = 8
D_NOPE_DIM = 64
D_ROPE_DIM = 32
D_V_DIM = 64
D_Q_RANK = 384
D_KV_RANK = 256
D_PAD_DIM = 128
ROPE_THETA = 10000.0
X_HEADS = 4
X_HEAD_DIM = 128
D_FF = 4 * D_MODEL
FF_CHUNK = 1024

ROW_TILE = 512
ATTN_TILE = ROW_TILE
VMEM_LIMIT = 56 * 1024 * 1024


def _cparams(sem):
    return pltpu.CompilerParams(dimension_semantics=sem, vmem_limit_bytes=VMEM_LIMIT)


def _rms(x, g):
    return x * lax.rsqrt(jnp.mean(x * x, axis=-1, keepdims=True) + EPS) * g


def _dot(a, b):
    return jnp.dot(a, b, preferred_element_type=F32)


def _dot_t(a, b):
    return lax.dot_general(a, b, (((1,), (1,)), ((), ())), preferred_element_type=F32)


def _const_spec(shape):
    nd = len(shape)
    return pl.BlockSpec(shape, lambda *_: (0,) * nd)


def _ab_in_kernel(x_ref, g_ref, wqkv_ref, wu_ref, wg_ref, *rest):
    views, glu_ref, qkv_scr = rest[:-2], rest[-2], rest[-1]
    n = _rms(x_ref[0], g_ref[...]).astype(BF16)
    width = 3 * A_WIDTH
    dense = [v for (_, d), v in zip(A_PATTERNS, views) if d == 1]
    chunk = 256
    assert A_WIDTH % chunk == 0
    for c0 in range(0, width, chunk):
        part = _dot(n, wqkv_ref[:, c0:c0 + chunk])
        if c0 < A_WIDTH:
            part = part * (A_HEAD_DIM ** -0.5 * LOG2E)
        for c in range(c0 // 128, (c0 + chunk) // 128):
            qkv_scr[c] = part[:, c * 128 - c0:(c + 1) * 128 - c0]
        for view_ref in dense:
            view_ref[0, :, c0:c0 + chunk] = part.astype(BF16)
    for (_, d), view_ref in zip(A_PATTERNS, views):
        if d == 1:
            continue
        for r in range(d):
            for c in range(width // 128):
                view_ref[0, :, r * width + c * 128:r * width + (c + 1) * 128] = (
                    qkv_scr[c, pl.ds(r, ROW_TILE // d, stride=d), :].astype(BF16))
    u = _dot(n, wu_ref[...])
    gate = _dot(n, wg_ref[...])
    glu_ref[0] = u * jax.nn.sigmoid(gate)


def _ab_in(x3, g, w_in):
    b, s, _ = x3.shape
    width = 3 * A_WIDTH
    wqkv = w_in[:, :width].astype(BF16)
    wu = w_in[:, width:width + B_CHANNELS].astype(BF16)
    wg = w_in[:, width + B_CHANNELS:].astype(BF16)
    row = lambda rows, w: pl.BlockSpec((1, rows, w), lambda bi, i: (bi, i, 0))
    dils = [d for _, d in A_PATTERNS]
    return pl.pallas_call(
        _ab_in_kernel,
        grid=(b, s // ROW_TILE),
        in_specs=[
            row(ROW_TILE, D_MODEL),
            _const_spec((1, D_MODEL)),
            _const_spec((D_MODEL, width)),
            _const_spec((D_MODEL, B_CHANNELS)),
            _const_spec((D_MODEL, B_CHANNELS)),
        ],
        out_specs=[row(ROW_TILE // d, d * width) for d in dils] + [row(ROW_TILE, B_CHANNELS)],
        out_shape=[jax.ShapeDtypeStruct((b, s // d, d * width), BF16) for d in dils]
        + [jax.ShapeDtypeStruct((b, s, B_CHANNELS), F32)],
        scratch_shapes=[pltpu.VMEM((width // 128, ROW_TILE, 128), F32)],
        compiler_params=_cparams(("parallel", "parallel")),
        name="ab_in_proj",
    )(x3, g.reshape(1, D_MODEL), wqkv, wu, wg)


def _dilated_kernel(q_ref, kp_ref, kc_ref, vp_ref, vc_ref, o_ref, lse_ref, s_ref, *, rows):
    first_tile = pl.program_id(2) == 0
    qi = lax.broadcasted_iota(jnp.int32, (A_BLK, 2 * A_BLK), 0)
    kj = lax.broadcasted_iota(jnp.int32, (A_BLK, 2 * A_BLK), 1)
    band = ((kj < A_BLK) & (kj >= qi)) | ((kj >= A_BLK) & (kj - A_BLK <= qi))
    lane = lax.broadcasted_iota(jnp.int32, (A_BLK, 128), 1)
    low = lane < A_HEAD_DIM
    pairs = A_WIDTH // 128
    nblk = rows // A_BLK

    def blk(i):
        return slice(i * A_BLK, (i + 1) * A_BLK)

    def prev_and_cur(i, prev_ref, cur_ref, cols):
        prev = prev_ref[0, :, cols] if i == 0 else cur_ref[0, blk(i - 1), cols]
        return jnp.concatenate([prev, cur_ref[0, blk(i), cols]], axis=0)

    def scores(i):
        for pr in range(pairs):
            cols = slice(pr * 128, (pr + 1) * 128)
            q = q_ref[0, blk(i), cols]
            k = prev_and_cur(i, kp_ref, kc_ref, cols)
            zero = jnp.zeros_like(q)
            s_ref[i % 2, 2 * pr] = _dot_t(jnp.where(low, q, zero), k)
            s_ref[i % 2, 2 * pr + 1] = _dot_t(jnp.where(low, zero, q), k)

    def consume(i):
        valid = band & ((kj >= A_BLK) | jnp.logical_not(first_tile)) if i == 0 else band
        for pr in range(pairs):
            cols = slice(pr * 128, (pr + 1) * 128)
            v = prev_and_cur(i, vp_ref, vc_ref, cols)
            outs, lses = [], []
            for half in range(2):
                s = jnp.where(valid, s_ref[i % 2, 2 * pr + half], NEG)
                m = jnp.max(s, axis=-1, keepdims=True)
                p = jnp.exp2(s - m)
                l = jnp.sum(p, axis=-1, keepdims=True)
                outs.append(_dot(p.astype(BF16), v) / l)
                lses.append(m + jnp.log2(l))
            o_ref[0, blk(i), cols] = jnp.where(low, outs[0], outs[1]).astype(BF16)
            lse_ref[0, blk(i), cols] = jnp.where(low, lses[0], lses[1])

    scores(0)
    for i in range(nblk):
        if i + 1 < nblk:
            scores(i + 1)
        consume(i)


def _dilated_pattern(view, dilation):
    b, sub, _ = view.shape
    rows = min(sub, 512)
    assert sub % rows == 0 and rows % A_BLK == 0
    blocks_per_tile = rows // A_BLK

    def cur_spec(part):
        return pl.BlockSpec((1, rows, A_WIDTH), lambda bi, r, n: (bi, n, 3 * r + part))

    def prev_spec(part):
        return pl.BlockSpec(
            (1, A_BLK, A_WIDTH),
            lambda bi, r, n: (bi, jnp.maximum(n * blocks_per_tile - 1, 0), 3 * r + part))

    out_spec = pl.BlockSpec((1, rows, A_WIDTH), lambda bi, r, n: (bi, n, r))
    return pl.pallas_call(
        functools.partial(_dilated_kernel, rows=rows),
        grid=(b, dilation, sub // rows),
        in_specs=[cur_spec(0), prev_spec(1), cur_spec(1), prev_spec(2), cur_spec(2)],
        out_specs=[out_spec, out_spec],
        out_shape=[
            jax.ShapeDtypeStruct((b, sub, dilation * A_WIDTH), BF16),
            jax.ShapeDtypeStruct((b, sub, dilation * A_WIDTH), F32),
        ],
        scratch_shapes=[pltpu.VMEM((2, A_HEADS, A_BLK, 2 * A_BLK), F32)],
        compiler_params=_cparams(("parallel", "parallel", "arbitrary")),
        name=f"dilated_attn_d{dilation}",
    )(view, view, view, view, view)


CONV_ROWS = 128
SUBLANES = 8


def _conv_kernel(halo_ref, cur_ref, w_ref, cb_ref, g_ref, b_ref, out_ref, buf_ref, *, rows):
    first_tile = pl.program_id(1) == 0
    halo = halo_ref[0]
    buf_ref[0:B_HALO, :] = jnp.where(first_tile, jnp.zeros_like(halo), halo)
    buf_ref[B_HALO:B_HALO + rows, :] = cur_ref[0]
    lead = B_HALO - (B_CONV_WIDTH - 1)

    def chunk(c, carry):
        base = pl.multiple_of(c * CONV_ROWS, CONV_ROWS)
        span = CONV_ROWS + B_HALO
        accs = []
        for c in range(B_CHANNELS // 128):
            cols = slice(c * 128, (c + 1) * 128)
            win = buf_ref[pl.ds(base, span), cols]
            acc = jnp.zeros((CONV_ROWS, 128), F32)
            for r in range(SUBLANES):
                shifted = pltpu.roll(win, span - r, 0) if r else win
                for tap in range(B_CONV_WIDTH):
                    if (lead + tap) % SUBLANES == r:
                        a = (lead + tap) // SUBLANES * SUBLANES
                        acc = acc + shifted[a:a + CONV_ROWS, :] * w_ref[tap:tap + 1, cols]
            accs.append(acc)
        y = jnp.concatenate(accs, axis=-1) + cb_ref[...]
        mu = jnp.mean(y, axis=-1, keepdims=True)
        yc = y - mu
        var = jnp.mean(yc * yc, axis=-1, keepdims=True)
        z = yc * lax.rsqrt(var + EPS) * g_ref[...] + b_ref[...]
        out_ref[0, pl.ds(base, CONV_ROWS), :] = (z * jax.nn.sigmoid(z)).astype(BF16)
        return carry

    lax.fori_loop(0, rows // CONV_ROWS, chunk, 0)


def _conv_module(glu, conv_w, conv_b, ln_g, ln_b):
    b, s, c = glu.shape
    rows = ROW_TILE
    halo_blocks = rows // B_HALO
    vec = lambda a: a.reshape(1, c)
    return pl.pallas_call(
        functools.partial(_conv_kernel, rows=rows),
        grid=(b, s // rows),
        in_specs=[
            pl.BlockSpec((1, B_HALO, c), lambda bi, n: (bi, jnp.maximum(n * halo_blocks - 1, 0), 0)),
            pl.BlockSpec((1, rows, c), lambda bi, n: (bi, n, 0)),
            _const_spec((B_CONV_WIDTH, c)),
            _const_spec((1, c)), _const_spec((1, c)), _const_spec((1, c)),
        ],
        out_specs=pl.BlockSpec((1, rows, c), lambda bi, n: (bi, n, 0)),
        out_shape=jax.ShapeDtypeStruct((b, s, c), BF16),
        scratch_shapes=[pltpu.VMEM((B_HALO + rows, c), F32)],
        compiler_params=_cparams(("parallel", "arbitrary")),
        name="conformer_conv",
    )(glu, glu, conv_w.reshape(B_CONV_WIDTH, c), vec(conv_b), vec(ln_g), vec(ln_b))


def _ab_out_kernel(x_ref, *rest):
    n_pat = len(A_PATTERNS)
    o_refs, l_refs = rest[:n_pat], rest[n_pat:2 * n_pat]
    yb_ref, wa_ref, wb_ref, out_ref = rest[2 * n_pat:2 * n_pat + 4]
    scr = list(rest[2 * n_pat + 4:])
    tiles = A_WIDTH // 128
    gather = lambda ref: jnp.concatenate([ref[c] for c in range(tiles)], axis=-1)
    outs, lses = [], []
    for idx, (_, d) in enumerate(A_PATTERNS):
        if d == 1:
            outs.append(o_refs[idx][0].astype(F32))
            lses.append(l_refs[idx][0])
            continue
        o_scr, l_scr = scr.pop(0), scr.pop(0)
        for r in range(d):
            dst = pl.ds(r, ROW_TILE // d, stride=d)
            for c in range(tiles):
                cols = slice(r * A_WIDTH + c * 128, r * A_WIDTH + (c + 1) * 128)
                o_scr[c, dst, :] = o_refs[idx][0, :, cols].astype(F32)
                l_scr[c, dst, :] = l_refs[idx][0, :, cols]
        outs.append(gather(o_scr))
        lses.append(gather(l_scr))
    m = functools.reduce(jnp.maximum, lses)
    es = [jnp.exp2(l - m) for l in lses]
    ya = sum(e * o for e, o in zip(es, outs)) / sum(es)
    out_ref[0] = (x_ref[0] + _dot(ya.astype(BF16), wa_ref[...])
                  + _dot(yb_ref[0], wb_ref[...]))


def _ab_out(x3, outs, lses, yb, w_out):
    b, s, _ = x3.shape
    row = lambda rows, w: pl.BlockSpec((1, rows, w), lambda bi, i: (bi, i, 0))
    views = [row(ROW_TILE // d, d * A_WIDTH) for _, d in A_PATTERNS]
    scratch = []
    for _, d in A_PATTERNS:
        if d > 1:
            scratch += [pltpu.VMEM((A_WIDTH // 128, ROW_TILE, 128), F32)] * 2
    return pl.pallas_call(
        _ab_out_kernel,
        grid=(b, s // ROW_TILE),
        in_specs=[row(ROW_TILE, D_MODEL)] + views + views + [
            row(ROW_TILE, B_CHANNELS),
            _const_spec((A_WIDTH, D_MODEL)), _const_spec((B_CHANNELS, D_MODEL))],
        out_specs=row(ROW_TILE, D_MODEL),
        out_shape=jax.ShapeDtypeStruct((b, s, D_MODEL), F32),
        scratch_shapes=scratch,
        compiler_params=_cparams(("parallel", "parallel")),
        name="ab_out_proj",
    )(x3, *outs, *lses, yb, w_out[:A_WIDTH].astype(BF16), w_out[A_WIDTH:].astype(BF16))


def _mem_kv_kernel(mem_ref, g_ref, wk_ref, wv_ref, k_ref, v_ref):
    n = _rms(mem_ref[...], g_ref[...]).astype(BF16)
    k_ref[...] = _dot(n, wk_ref[...]).astype(BF16)
    v_ref[...] = _dot(n, wv_ref[...]).astype(BF16)


def _mem_kv(mem2, g, wkv):
    rows = mem2.shape[0]
    w = X_HEADS * X_HEAD_DIM
    return pl.pallas_call(
        _mem_kv_kernel,
        grid=(1,),
        in_specs=[_const_spec((rows, D_MODEL)), _const_spec((1, D_MODEL)),
                  _const_spec((D_MODEL, w)), _const_spec((D_MODEL, w))],
        out_specs=[_const_spec((rows, w)), _const_spec((rows, w))],
        out_shape=[jax.ShapeDtypeStruct((rows, w), BF16)] * 2,
        compiler_params=_cparams(("arbitrary",)),
        name="mem_kv_proj",
    )(mem2, g.reshape(1, D_MODEL), wkv[:, :w].astype(BF16), wkv[:, w:].astype(BF16))


def _cross_kernel(x_ref, g_ref, wq_ref, k_ref, v_ref, wo_ref, out_ref):
    x = x_ref[0]
    n = _rms(x, g_ref[...]).astype(BF16)
    q = (_dot(n, wq_ref[...]) * (X_HEAD_DIM ** -0.5)).astype(BF16)
    heads = []
    for h in range(X_HEADS):
        cols = slice(h * X_HEAD_DIM, (h + 1) * X_HEAD_DIM)
        s = _dot_t(q[:, cols], k_ref[0, :, cols])
        m = jnp.max(s, axis=-1, keepdims=True)
        p = jnp.exp(s - m)
        l = jnp.sum(p, axis=-1, keepdims=True)
        heads.append((_dot(p.astype(BF16), v_ref[0, :, cols]) / l).astype(BF16))
    o = jnp.concatenate(heads, axis=-1)
    out_ref[0] = x + _dot(o, wo_ref[...])


def _cross_attention(x3, g, wq, k, v, wo):
    b, s, _ = x3.shape
    m = k.shape[1]
    w = X_HEADS * X_HEAD_DIM
    return pl.pallas_call(
        _cross_kernel,
        grid=(b, s // ROW_TILE),
        in_specs=[
            pl.BlockSpec((1, ROW_TILE, D_MODEL), lambda bi, i: (bi, i, 0)),
            _const_spec((1, D_MODEL)),
            _const_spec((D_MODEL, w)),
            pl.BlockSpec((1, m, w), lambda bi, i: (bi, 0, 0)),
            pl.BlockSpec((1, m, w), lambda bi, i: (bi, 0, 0)),
            _const_spec((w, D_MODEL)),
        ],
        out_specs=pl.BlockSpec((1, ROW_TILE, D_MODEL), lambda bi, i: (bi, i, 0)),
        out_shape=jax.ShapeDtypeStruct((b, s, D_MODEL), F32),
        compiler_params=_cparams(("parallel", "parallel")),
        name="cross_attn",
    )(x3, g.reshape(1, D_MODEL), wq.astype(BF16), k, v, wo.astype(BF16))


def _mlp_kernel(x_ref, g_ref, w1_ref, w2_ref, gf_ref, out_ref, *, final_norm):
    x = x_ref[...]
    n = _rms(x, g_ref[...]).astype(BF16)
    acc = x
    for c in range(D_FF // FF_CHUNK):
        cols = slice(c * FF_CHUNK, (c + 1) * FF_CHUNK)
        h = jnp.maximum(_dot(n, w1_ref[:, cols]), 0.0)
        acc = acc + _dot((h * h).astype(BF16), w2_ref[cols, :])
    if final_norm:
        acc = _rms(acc, gf_ref[...])
    out_ref[...] = acc


def _mlp(x2, g, w1, w2, gf, final_norm):
    t = x2.shape[0]
    return pl.pallas_call(
        functools.partial(_mlp_kernel, final_norm=final_norm),
        grid=(t // ROW_TILE,),
        in_specs=[
            pl.BlockSpec((ROW_TILE, D_MODEL), lambda i: (i, 0)),
            _const_spec((1, D_MODEL)),
            _const_spec((D_MODEL, D_FF)),
            _const_spec((D_FF, D_MODEL)),
            _const_spec((1, D_MODEL)),
        ],
        out_specs=pl.BlockSpec((ROW_TILE, D_MODEL), lambda i: (i, 0)),
        out_shape=jax.ShapeDtypeStruct((t, D_MODEL), F32),
        compiler_params=_cparams(("parallel",)),
        name="mlp_final" if final_norm else "mlp",
    )(x2, g.reshape(1, D_MODEL), w1.astype(BF16), w2.astype(BF16), gf.reshape(1, D_MODEL))


def _rope_rows(xt, cos, sin):
    lo, half = D_NOPE_DIM, D_ROPE_DIM // 2
    x1, x2 = xt[lo:lo + half], xt[lo + half:lo + 2 * half]
    return jnp.concatenate(
        [xt[:lo], x1 * cos - x2 * sin, x2 * cos + x1 * sin, xt[lo + 2 * half:]], axis=0)


def _cd_in_kernel(x_ref, pos_ref, g_ref, freq_ref, wqkv_ref, wcq_ref, wckv_ref, wkr_ref,
                  qng_ref, kvng_ref, wuq_ref, wuk_ref, wuv_ref,
                  qct_ref, kc_ref, vct_ref, qdt_ref, kd_ref, vdt_ref):
    n = _rms(x_ref[0], g_ref[...]).astype(BF16)
    qc = _dot(n, wqkv_ref[:, :C_WIDTH])
    qct_ref[0, 0] = (qc * (C_HEAD_DIM ** -0.5 * LOG2E)).T.astype(BF16)
    vct_ref[0, 0] = _dot(n, wqkv_ref[:, 2 * C_WIDTH:]).T.astype(BF16)

    ang = freq_ref[...] * pos_ref[0, 0].astype(F32)
    cos, sin = jnp.cos(ang), jnp.sin(ang)

    cq = _rms(_dot(n, wcq_ref[...]), qng_ref[...]).astype(BF16)
    q = _dot(cq, wuq_ref[...])
    ckv = _rms(_dot(n, wckv_ref[...]), kvng_ref[...]).astype(BF16)
    k_nope = _dot(ckv, wuk_ref[...])
    vdt_ref[0, 0] = _dot(ckv, wuv_ref[...]).T.astype(BF16)
    k_rope = _rope_rows(_dot(n, wkr_ref[...]).T, cos, sin).T
    scale = (D_NOPE_DIM + D_ROPE_DIM) ** -0.5 * LOG2E
    for h in range(D_HEADS):
        cols = slice(h * D_PAD_DIM, (h + 1) * D_PAD_DIM)
        qdt_ref[0, 0, cols, :] = (_rope_rows(q[:, cols].T, cos, sin) * scale).astype(BF16)
        kd_ref[0, :, cols] = (k_nope[:, cols] + k_rope).astype(BF16)
    kc_ref[0] = _dot(n, wqkv_ref[:, C_WIDTH:2 * C_WIDTH]).astype(BF16)


def _pad_heads(w, heads, width):
    k = w.shape[0]
    w = w.reshape(k, heads, width)
    return jnp.pad(w, ((0, 0), (0, 0), (0, D_PAD_DIM - width))).reshape(k, heads * D_PAD_DIM)


def _cd_in(x3, positions, g, w_in, q_norm_g, kv_norm_g, w_uq, w_uk, w_uv):
    b, s, _ = x3.shape
    o3 = 3 * C_WIDTH
    o4 = o3 + D_Q_RANK
    o5 = o4 + D_KV_RANK
    wqkv = w_in[:, :o3].astype(BF16)
    wcq = w_in[:, o3:o4].astype(BF16)
    wckv = w_in[:, o4:o5].astype(BF16)
    wkr = jnp.pad(w_in[:, o5:], ((0, 0), (D_NOPE_DIM, D_PAD_DIM - D_NOPE_DIM - D_ROPE_DIM))).astype(BF16)
    wuq = _pad_heads(w_uq, D_HEADS, D_NOPE_DIM + D_ROPE_DIM).astype(BF16)
    wuk = _pad_heads(w_uk, D_HEADS, D_NOPE_DIM).astype(BF16)
    half = D_ROPE_DIM // 2
    freq = (ROPE_THETA ** (-jnp.arange(half, dtype=F32) / half)).reshape(half, 1)
    dw = D_HEADS * D_PAD_DIM
    nt = s // ROW_TILE
    row = lambda w: pl.BlockSpec((1, ROW_TILE, w), lambda bi, i: (bi, i, 0))
    colmajor = lambda w: pl.BlockSpec((1, 1, w, ROW_TILE), lambda bi, i: (bi, i, 0, 0))
    return pl.pallas_call(
        _cd_in_kernel,
        grid=(b, s // ROW_TILE),
        in_specs=[
            row(D_MODEL), colmajor(1), _const_spec((1, D_MODEL)), _const_spec((half, 1)),
            _const_spec((D_MODEL, o3)), _const_spec((D_MODEL, D_Q_RANK)),
            _const_spec((D_MODEL, D_KV_RANK)), _const_spec((D_MODEL, D_PAD_DIM)),
            _const_spec((1, D_Q_RANK)), _const_spec((1, D_KV_RANK)),
            _const_spec((D_Q_RANK, dw)), _const_spec((D_KV_RANK, dw)),
            _const_spec((D_KV_RANK, D_HEADS * D_V_DIM)),
        ],
        out_specs=[colmajor(C_WIDTH), row(C_WIDTH), colmajor(C_WIDTH), colmajor(dw), row(dw),
                   colmajor(D_HEADS * D_V_DIM)],
        out_shape=[
            jax.ShapeDtypeStruct((b, nt, C_WIDTH, ROW_TILE), BF16),
            jax.ShapeDtypeStruct((b, s, C_WIDTH), BF16),
            jax.ShapeDtypeStruct((b, nt, C_WIDTH, ROW_TILE), BF16),
            jax.ShapeDtypeStruct((b, nt, dw, ROW_TILE), BF16),
            jax.ShapeDtypeStruct((b, s, dw), BF16),
            jax.ShapeDtypeStruct((b, nt, D_HEADS * D_V_DIM, ROW_TILE), BF16),
        ],
        compiler_params=_cparams(("parallel", "parallel")),
        name="cd_in_proj",
    )(x3, positions.reshape(b, nt, 1, ROW_TILE), g.reshape(1, D_MODEL), freq, wqkv, wcq, wckv, wkr,
      q_norm_g.reshape(1, D_Q_RANK), kv_norm_g.reshape(1, D_KV_RANK), wuq, wuk,
      w_uv.astype(BF16))


def _causal_kernel(qt_ref, k_ref, vt_ref, *rest, mode, lam_init, n_tiles):
    if mode == "diff":
        lq1_ref, lk1_ref, lq2_ref, lk2_ref, g_ref, out_ref = rest[:6]
    else:
        out_ref = rest[0]
    s_bufs, mx_bufs = rest[-11:-7], rest[-7:-3]
    m_ref, l_ref, acc_ref = rest[-3:]
    t = ATTN_TILE
    n_pairs = n_tiles * (n_tiles + 1) // 2
    assert n_tiles >= 2 and n_pairs % 2 == 0
    feat = lax.broadcasted_iota(jnp.int32, (128, t), 0)

    def score(pair, buf, j):
        qi, kb = pair
        rows = pl.ds(pl.multiple_of(kb * t, t), t)
        if mode == "diff":
            qt = qt_ref[0, qi]
            mine = (feat < C_HEAD_DIM) if j == 0 else (feat >= C_HEAD_DIM)
            q, k = jnp.where(mine, qt, jnp.zeros_like(qt)), k_ref[0, rows, :]
        else:
            cols = slice(j * D_PAD_DIM, (j + 1) * D_PAD_DIM)
            q, k = qt_ref[0, qi, cols, :], k_ref[0, rows, cols]
        s = _dot(k, q)
        s_bufs[buf][j] = s
        mx_bufs[buf][j] = jnp.max(s, axis=0, keepdims=True)

    def finish_tile(qi):
        a1 = acc_ref[0] / l_ref[0]
        a2 = acc_ref[1] / l_ref[1]
        if mode == "diff":
            lam = (jnp.exp(jnp.sum(lq1_ref[...] * lk1_ref[...], axis=-1, keepdims=True))
                   - jnp.exp(jnp.sum(lq2_ref[...] * lk2_ref[...], axis=-1, keepdims=True))
                   + lam_init)
            d = a1 - lam * a2
            out = (d * lax.rsqrt(jnp.mean(d * d, axis=0, keepdims=True) + EPS) * g_ref[...]
                   * (1.0 - lam_init))
        else:
            out = jnp.concatenate([a1, a2], axis=0)
        out_ref[0, pl.ds(pl.multiple_of(qi * t, t), t), :] = out.T.astype(BF16)

    def consume(pair, buf, j, diagonal):
        qi, kb = pair
        first = kb == 0
        s = s_bufs[buf][j]
        if diagonal:
            kr = lax.broadcasted_iota(jnp.int32, (t, t), 0)
            qc = lax.broadcasted_iota(jnp.int32, (t, t), 1)
            s = jnp.where(kr <= qc, s, NEG)
            block_max = jnp.max(s, axis=0, keepdims=True)
        else:
            block_max = mx_bufs[buf][j]
        m = jnp.where(first, NEG, m_ref[j])
        l = jnp.where(first, 0.0, l_ref[j])
        m_new = jnp.maximum(m, block_max)
        alpha = jnp.exp2(m - m_new)
        p = jnp.exp2(s - m_new)
        m_ref[j] = m_new
        l_ref[j] = alpha * l + jnp.sum(p, axis=0, keepdims=True)
        if mode == "diff":
            vt = vt_ref[0, kb]
        else:
            vt = vt_ref[0, kb, j * D_V_DIM:(j + 1) * D_V_DIM, :]
        acc_ref[j] = alpha * acc_ref[j] + _dot(vt, p.astype(BF16))
        if diagonal and j == 1:
            finish_tile(qi)

    def advance(pair):
        qi, kb = pair
        end = kb == qi
        nqi, nkb = jnp.where(end, qi + 1, qi), jnp.where(end, 0, kb + 1)
        done = nqi >= n_tiles
        return jnp.where(done, qi, nqi), jnp.where(done, kb, nkb)

    acc_ref[...] = jnp.zeros(acc_ref.shape, F32)
    pair0 = (jnp.int32(0), jnp.int32(0))
    pair1 = advance(pair0)
    for j in range(2):
        score(pair0, 0, j)
        score(pair1, 1, j)

    def body(trip, carry):
        pa, pb = carry[:2], carry[2:]
        pc = advance(pb)
        pd = advance(pc)
        diag_a, diag_b = pa[0] == pa[1], pb[0] == pb[1]
        for half in range(2):
            ra, rb, wc, wd = 2 * half, 2 * half + 1, 2 - 2 * half, 3 - 2 * half
            for da, db, cond in ((False, False, jnp.logical_not(diag_a | diag_b)),
                                 (True, False, diag_a), (False, True, diag_b)):
                @pl.when(cond & (trip % 2 == half))
                def _():
                    score(pc, wc, 0)
                    score(pc, wc, 1)
                    consume(pa, ra, 0, da)
                    score(pd, wd, 0)
                    consume(pa, ra, 1, da)
                    consume(pb, rb, 0, db)
                    score(pd, wd, 1)
                    consume(pb, rb, 1, db)
        return (*pc, *pd)

    lax.fori_loop(0, n_pairs // 2, body, (*pair0, *pair1))


def _causal_attention(qt, k, vt, mode, extras=(), lam_init=0.0):
    b, s, _ = k.shape
    t = ATTN_TILE
    assert qt.shape[3] == t and vt.shape[3] == t
    qw = 128 if mode == "diff" else 2 * D_PAD_DIM
    groups = k.shape[2] // qw
    in_specs = [
        pl.BlockSpec((1, s // t, qw, t), lambda bi, h: (bi, 0, h, 0)),
        pl.BlockSpec((1, s, qw), lambda bi, h: (bi, 0, h)),
        pl.BlockSpec((1, s // t, 128, t), lambda bi, h: (bi, 0, h, 0)),
    ] + [_const_spec(e.shape) for e in extras]
    stats = [pltpu.VMEM((2, 1, t), F32)] * 2
    return pl.pallas_call(
        functools.partial(_causal_kernel, mode=mode, lam_init=lam_init, n_tiles=s // t),
        grid=(b, groups),
        in_specs=in_specs,
        out_specs=pl.BlockSpec((1, s, 128), lambda bi, h: (bi, 0, h)),
        out_shape=jax.ShapeDtypeStruct((b, s, groups * 128), BF16),
        scratch_shapes=[
            *[pltpu.VMEM((2, t, t), F32)] * 4,
            *stats, *stats,
            *stats,
            pltpu.VMEM((2, 128 if mode == "diff" else D_V_DIM, t), F32),
        ],
        compiler_params=_cparams(("parallel", "parallel")),
        name=f"causal_attn_{mode}",
    )(qt, k, vt, *extras)


def _cd_out_kernel(x_ref, yc_ref, yd_ref, wc_ref, wd_ref, out_ref):
    out_ref[...] = x_ref[...] + _dot(yc_ref[...], wc_ref[...]) + _dot(yd_ref[...], wd_ref[...])


def _cd_out(x2, yc, yd, w_out):
    t = x2.shape[0]
    row = lambda w: pl.BlockSpec((ROW_TILE, w), lambda i: (i, 0))
    dvw = D_HEADS * D_V_DIM
    return pl.pallas_call(
        _cd_out_kernel,
        grid=(t // ROW_TILE,),
        in_specs=[row(D_MODEL), row(C_WIDTH), row(dvw),
                  _const_spec((C_WIDTH, D_MODEL)), _const_spec((dvw, D_MODEL))],
        out_specs=row(D_MODEL),
        out_shape=jax.ShapeDtypeStruct((t, D_MODEL), F32),
        compiler_params=_cparams(("parallel",)),
        name="cd_out_proj",
    )(x2, yc, yd, w_out[:C_WIDTH].astype(BF16), w_out[C_WIDTH:].astype(BF16))


def kernel(x, mem, positions, norm_mix_g, norm_cross_g, norm_mem_g, cross_wq, cross_wkv, cross_wo,
           norm_mlp_g, mlp_w1, mlp_w2, ab_w_in, ab_w_out, ab_conv_w, ab_conv_b, ab_ln_g, ab_ln_b,
           cd_w_in, cd_w_out, diff_lq1, diff_lk1, diff_lq2, diff_lk2, diff_subln_g, mla_q_norm_g,
           mla_kv_norm_g, mla_w_uq, mla_w_uk, mla_w_uv, final_norm_g):
    b, s, d = x.shape
    t = b * s
    mem_len = mem.shape[1]
    depth = norm_mix_g.shape[0]
    xw = X_HEADS * X_HEAD_DIM
    x2 = x.reshape(t, d)
    for i in range(depth):
        j = i // 2
        if i % 2 == 0:
            *views, glu = _ab_in(x2.reshape(b, s, d), norm_mix_g[i], ab_w_in[j])
            outs, lses = zip(*[_dilated_pattern(view, dil)
                               for view, (_, dil) in zip(views, A_PATTERNS)])
            yb = _conv_module(glu, ab_conv_w[j], ab_conv_b[j], ab_ln_g[j], ab_ln_b[j])
            x2 = _ab_out(x2.reshape(b, s, d), outs, lses, yb, ab_w_out[j]).reshape(t, d)
        else:
            qc, kc, vc, qd, kd, vd = _cd_in(
                x2.reshape(b, s, d), positions, norm_mix_g[i], cd_w_in[j], mla_q_norm_g[j],
                mla_kv_norm_g[j], mla_w_uq[j], mla_w_uk[j], mla_w_uv[j])
            lam_init = 0.8 - 0.6 * math.exp(-0.3 * i)
            vec = lambda a: a.reshape(1, -1)
            yc = _causal_attention(
                qc, kc, vc, "diff",
                extras=(vec(diff_lq1[j]), vec(diff_lk1[j]), vec(diff_lq2[j]), vec(diff_lk2[j]),
                        diff_subln_g[j].reshape(-1, 1)),
                lam_init=lam_init)
            yd = _causal_attention(qd, kd, vd, "mla")
            x2 = _cd_out(x2, yc.reshape(t, C_WIDTH), yd.reshape(t, D_HEADS * D_V_DIM), cd_w_out[j])
        km, vm = _mem_kv(mem.reshape(b * mem_len, d), norm_mem_g[i], cross_wkv[i])
        x2 = _cross_attention(x2.reshape(b, s, d), norm_cross_g[i], cross_wq[i],
                              km.reshape(b, mem_len, xw), vm.reshape(b, mem_len, xw),
                              cross_wo[i]).reshape(t, d)
        x2 = _mlp(x2, norm_mlp_g[i], mlp_w1[i], mlp_w2[i], final_norm_g, i == depth - 1)
    return x2.reshape(b, s, d)
```

```python
import functools
import math

import jax
import jax.numpy as jnp
from jax import lax
from jax.experimental import pallas as pl
from jax.experimental.pallas import tpu as pltpu

F32 = jnp.float32
BF16 = jnp.bfloat16

D_MODEL = 1024
EPS = 1e-6
LOG2E = math.log2(math.e)
NEG = -1e30

A_HEADS = 8
A_HEAD_DIM = 64
A_WIDTH = A_HEADS * A_HEAD_DIM
A_PATTERNS = ((128, 1), (512, 4), (2048, 16))
A_BLK = 128
assert all(window // dilation == A_BLK for window, dilation in A_PATTERNS)
B_CHANNELS = 512
B_CONV_WIDTH = 31
B_HALO = 32
C_HEADS = 4
C_HEAD_DIM = 64
C_WIDTH = C_HEADS * 2 * C_HEAD_DIM
D_HEADS = 8
D_NOPE_DIM = 64
D_ROPE_DIM = 32
D_V_DIM = 64
D_Q_RANK = 384
D_KV_RANK = 256
D_PAD_DIM = 128
ROPE_THETA = 10000.0
X_HEADS = 4
X_HEAD_DIM = 128
D_FF = 4 * D_MODEL
FF_CHUNK = 1024

ROW_TILE = 512
ATTN_TILE = ROW_TILE
VMEM_LIMIT = 56 * 1024 * 1024


def _cparams(sem):
    return pltpu.CompilerParams(dimension_semantics=sem, vmem_limit_bytes=VMEM_LIMIT)


def _rms(x, g):
    return x * lax.rsqrt(jnp.mean(x * x, axis=-1, keepdims=True) + EPS) * g


def _dot(a, b):
    return jnp.dot(a, b, preferred_element_type=F32)


def _dot_t(a, b):
    return lax.dot_general(a, b, (((1,), (1,)), ((), ())), preferred_element_type=F32)


def _const_spec(shape):
    nd = len(shape)
    return pl.BlockSpec(shape, lambda *_: (0,) * nd)


def _ab_in_kernel(x_ref, g_ref, wqkv_ref, wu_ref, wg_ref, *rest):
    views, glu_ref, qkv_scr = rest[:-2], rest[-2], rest[-1]
    n = _rms(x_ref[0], g_ref[...]).astype(BF16)
    width = 3 * A_WIDTH
    dense = [v for (_, d), v in zip(A_PATTERNS, views) if d == 1]
    chunk = 256
    assert A_WIDTH % chunk == 0
    for c0 in range(0, width, chunk):
        part = _dot(n, wqkv_ref[:, c0:c0 + chunk])
        if c0 < A_WIDTH:
            part = part * (A_HEAD_DIM ** -0.5 * LOG2E)
        for c in range(c0 // 128, (c0 + chunk) // 128):
            qkv_scr[c] = part[:, c * 128 - c0:(c + 1) * 128 - c0]
        for view_ref in dense:
            view_ref[0, :, c0:c0 + chunk] = part.astype(BF16)
    for (_, d), view_ref in zip(A_PATTERNS, views):
        if d == 1:
            continue
        for r in range(d):
            for c in range(width // 128):
                view_ref[0, :, r * width + c * 128:r * width + (c + 1) * 128] = (
                    qkv_scr[c, pl.ds(r, ROW_TILE // d, stride=d), :].astype(BF16))
    u = _dot(n, wu_ref[...])
    gate = _dot(n, wg_ref[...])
    glu_ref[0] = u * jax.nn.sigmoid(gate)


def _ab_in(x3, g, w_in):
    b, s, _ = x3.shape
    width = 3 * A_WIDTH
    wqkv = w_in[:, :width].astype(BF16)
    wu = w_in[:, width:width + B_CHANNELS].astype(BF16)
    wg = w_in[:, width + B_CHANNELS:].astype(BF16)
    row = lambda rows, w: pl.BlockSpec((1, rows, w), lambda bi, i: (bi, i, 0))
    dils = [d for _, d in A_PATTERNS]
    return pl.pallas_call(
        _ab_in_kernel,
        grid=(b, s // ROW_TILE),
        in_specs=[
            row(ROW_TILE, D_MODEL),
            _const_spec((1, D_MODEL)),
            _const_spec((D_MODEL, width)),
            _const_spec((D_MODEL, B_CHANNELS)),
            _const_spec((D_MODEL, B_CHANNELS)),
        ],
        out_specs=[row(ROW_TILE // d, d * width) for d in dils] + [row(ROW_TILE, B_CHANNELS)],
        out_shape=[jax.ShapeDtypeStruct((b, s // d, d * width), BF16) for d in dils]
        + [jax.ShapeDtypeStruct((b, s, B_CHANNELS), F32)],
        scratch_shapes=[pltpu.VMEM((width // 128, ROW_TILE, 128), F32)],
        compiler_params=_cparams(("parallel", "parallel")),
        name="ab_in_proj",
    )(x3, g.reshape(1, D_MODEL), wqkv, wu, wg)


def _dilated_kernel(q_ref, kp_ref, kc_ref, vp_ref, vc_ref, o_ref, lse_ref, s_ref, *, rows):
    first_tile = pl.program_id(2) == 0
    qi = lax.broadcasted_iota(jnp.int32, (A_BLK, 2 * A_BLK), 0)
    kj = lax.broadcasted_iota(jnp.int32, (A_BLK, 2 * A_BLK), 1)
    band = ((kj < A_BLK) & (kj >= qi)) | ((kj >= A_BLK) & (kj - A_BLK <= qi))
    lane = lax.broadcasted_iota(jnp.int32, (A_BLK, 128), 1)
    low = lane < A_HEAD_DIM
    pairs = A_WIDTH // 128
    nblk = rows // A_BLK

    def blk(i):
        return slice(i * A_BLK, (i + 1) * A_BLK)

    def prev_and_cur(i, prev_ref, cur_ref, cols):
        prev = prev_ref[0, :, cols] if i == 0 else cur_ref[0, blk(i - 1), cols]
        return jnp.concatenate([prev, cur_ref[0, blk(i), cols]], axis=0)

    def scores(i):
        for pr in range(pairs):
            cols = slice(pr * 128, (pr + 1) * 128)
            q = q_ref[0, blk(i), cols]
            k = prev_and_cur(i, kp_ref, kc_ref, cols)
            zero = jnp.zeros_like(q)
            s_ref[i % 2, 2 * pr] = _dot_t(jnp.where(low, q, zero), k)
            s_ref[i % 2, 2 * pr + 1] = _dot_t(jnp.where(low, zero, q), k)

    def consume(i):
        valid = band & ((kj >= A_BLK) | jnp.logical_not(first_tile)) if i == 0 else band
        for pr in range(pairs):
            cols = slice(pr * 128, (pr + 1) * 128)
            v = prev_and_cur(i, vp_ref, vc_ref, cols)
            outs, lses = [], []
            for half in range(2):
                s = jnp.where(valid, s_ref[i % 2, 2 * pr + half], NEG)
                m = jnp.max(s, axis=-1, keepdims=True)
                p = jnp.exp2(s - m)
                l = jnp.sum(p, axis=-1, keepdims=True)
                outs.append(_dot(p.astype(BF16), v) / l)
                lses.append(m + jnp.log2(l))
            o_ref[0, blk(i), cols] = jnp.where(low, outs[0], outs[1]).astype(BF16)
            lse_ref[0, blk(i), cols] = jnp.where(low, lses[0], lses[1])

    scores(0)
    for i in range(nblk):
        if i + 1 < nblk:
            scores(i + 1)
        consume(i)


def _dilated_pattern(view, dilation):
    b, sub, _ = view.shape
    rows = min(sub, 512)
    assert sub % rows == 0 and rows % A_BLK == 0
    blocks_per_tile = rows // A_BLK

    def cur_spec(part):
        return pl.BlockSpec((1, rows, A_WIDTH), lambda bi, r, n: (bi, n, 3 * r + part))

    def prev_spec(part):
        return pl.BlockSpec(
            (1, A_BLK, A_WIDTH),
            lambda bi, r, n: (bi, jnp.maximum(n * blocks_per_tile - 1, 0), 3 * r + part))

    out_spec = pl.BlockSpec((1, rows, A_WIDTH), lambda bi, r, n: (bi, n, r))
    return pl.pallas_call(
        functools.partial(_dilated_kernel, rows=rows),
        grid=(b, dilation, sub // rows),
        in_specs=[cur_spec(0), prev_spec(1), cur_spec(1), prev_spec(2), cur_spec(2)],
        out_specs=[out_spec, out_spec],
        out_shape=[
            jax.ShapeDtypeStruct((b, sub, dilation * A_WIDTH), BF16),
            jax.ShapeDtypeStruct((b, sub, dilation * A_WIDTH), F32),
        ],
        scratch_shapes=[pltpu.VMEM((2, A_HEADS, A_BLK, 2 * A_BLK), F32)],
        compiler_params=_cparams(("parallel", "parallel", "arbitrary")),
        name=f"dilated_attn_d{dilation}",
    )(view, view, view, view, view)


CONV_ROWS = 128
SUBLANES = 8


def _conv_kernel(halo_ref, cur_ref, w_ref, cb_ref, g_ref, b_ref, out_ref, buf_ref, *, rows):
    first_tile = pl.program_id(1) == 0
    halo = halo_ref[0]
    buf_ref[0:B_HALO, :] = jnp.where(first_tile, jnp.zeros_like(halo), halo)
    buf_ref[B_HALO:B_HALO + rows, :] = cur_ref[0]
    lead = B_HALO - (B_CONV_WIDTH - 1)

    def chunk(c, carry):
        base = pl.multiple_of(c * CONV_ROWS, CONV_ROWS)
        span = CONV_ROWS + B_HALO
        accs = []
        for c in range(B_CHANNELS // 128):
            cols = slice(c * 128, (c + 1) * 128)
            win = buf_ref[pl.ds(base, span), cols]
            acc = jnp.zeros((CONV_ROWS, 128), F32)
            for r in range(SUBLANES):
                shifted = pltpu.roll(win, span - r, 0) if r else win
                for tap in range(B_CONV_WIDTH):
                    if (lead + tap) % SUBLANES == r:
                        a = (lead + tap) // SUBLANES * SUBLANES
                        acc = acc + shifted[a:a + CONV_ROWS, :] * w_ref[tap:tap + 1, cols]
            accs.append(acc)
        y = jnp.concatenate(accs, axis=-1) + cb_ref[...]
        mu = jnp.mean(y, axis=-1, keepdims=True)
        yc = y - mu
        var = jnp.mean(yc * yc, axis=-1, keepdims=True)
        z = yc * lax.rsqrt(var + EPS) * g_ref[...] + b_ref[...]
        out_ref[0, pl.ds(base, CONV_ROWS), :] = (z * jax.nn.sigmoid(z)).astype(BF16)
        return carry

    lax.fori_loop(0, rows // CONV_ROWS, chunk, 0)


def _conv_module(glu, conv_w, conv_b, ln_g, ln_b):
    b, s, c = glu.shape
    rows = ROW_TILE
    halo_blocks = rows // B_HALO
    vec = lambda a: a.reshape(1, c)
    return pl.pallas_call(
        functools.partial(_conv_kernel, rows=rows),
        grid=(b, s // rows),
        in_specs=[
            pl.BlockSpec((1, B_HALO, c), lambda bi, n: (bi, jnp.maximum(n * halo_blocks - 1, 0), 0)),
            pl.BlockSpec((1, rows, c), lambda bi, n: (bi, n, 0)),
            _const_spec((B_CONV_WIDTH, c)),
            _const_spec((1, c)), _const_spec((1, c)), _const_spec((1, c)),
        ],
        out_specs=pl.BlockSpec((1, rows, c), lambda bi, n: (bi, n, 0)),
        out_shape=jax.ShapeDtypeStruct((b, s, c), BF16),
        scratch_shapes=[pltpu.VMEM((B_HALO + rows, c), F32)],
        compiler_params=_cparams(("parallel", "arbitrary")),
        name="conformer_conv",
    )(glu, glu, conv_w.reshape(B_CONV_WIDTH, c), vec(conv_b), vec(ln_g), vec(ln_b))


def _ab_out_kernel(x_ref, *rest):
    n_pat = len(A_PATTERNS)
    o_refs, l_refs = rest[:n_pat], rest[n_pat:2 * n_pat]
    yb_ref, wa_ref, wb_ref, out_ref = rest[2 * n_pat:2 * n_pat + 4]
    scr = list(rest[2 * n_pat + 4:])
    tiles = A_WIDTH // 128
    gather = lambda ref: jnp.concatenate([ref[c] for c in range(tiles)], axis=-1)
    outs, lses = [], []
    for idx, (_, d) in enumerate(A_PATTERNS):
        if d == 1:
            outs.append(o_refs[idx][0].astype(F32))
            lses.append(l_refs[idx][0])
            continue
        o_scr, l_scr = scr.pop(0), scr.pop(0)
        for r in range(d):
            dst = pl.ds(r, ROW_TILE // d, stride=d)
            for c in range(tiles):
                cols = slice(r * A_WIDTH + c * 128, r * A_WIDTH + (c + 1) * 128)
                o_scr[c, dst, :] = o_refs[idx][0, :, cols].astype(F32)
                l_scr[c, dst, :] = l_refs[idx][0, :, cols]
        outs.append(gather(o_scr))
        lses.append(gather(l_scr))
    m = functools.reduce(jnp.maximum, lses)
    es = [jnp.exp2(l - m) for l in lses]
    ya = sum(e * o for e, o in zip(es, outs)) / sum(es)
    out_ref[0] = (x_ref[0] + _dot(ya.astype(BF16), wa_ref[...])
                  + _dot(yb_ref[0], wb_ref[...]))


def _ab_out(x3, outs, lses, yb, w_out):
    b, s, _ = x3.shape
    row = lambda rows, w: pl.BlockSpec((1, rows, w), lambda bi, i: (bi, i, 0))
    views = [row(ROW_TILE // d, d * A_WIDTH) for _, d in A_PATTERNS]
    scratch = []
    for _, d in A_PATTERNS:
        if d > 1:
            scratch += [pltpu.VMEM((A_WIDTH // 128, ROW_TILE, 128), F32)] * 2
    return pl.pallas_call(
        _ab_out_kernel,
        grid=(b, s // ROW_TILE),
        in_specs=[row(ROW_TILE, D_MODEL)] + views + views + [
            row(ROW_TILE, B_CHANNELS),
            _const_spec((A_WIDTH, D_MODEL)), _const_spec((B_CHANNELS, D_MODEL))],
        out_specs=row(ROW_TILE, D_MODEL),
        out_shape=jax.ShapeDtypeStruct((b, s, D_MODEL), F32),
        scratch_shapes=scratch,
        compiler_params=_cparams(("parallel", "parallel")),
        name="ab_out_proj",
    )(x3, *outs, *lses, yb, w_out[:A_WIDTH].astype(BF16), w_out[A_WIDTH:].astype(BF16))


def _mem_kv_kernel(mem_ref, g_ref, wk_ref, wv_ref, k_ref, v_ref):
    n = _rms(mem_ref[...], g_ref[...]).astype(BF16)
    k_ref[...] = _dot(n, wk_ref[...]).astype(BF16)
    v_ref[...] = _dot(n, wv_ref[...]).astype(BF16)


def _mem_kv(mem2, g, wkv):
    rows = mem2.shape[0]
    w = X_HEADS * X_HEAD_DIM
    return pl.pallas_call(
        _mem_kv_kernel,
        grid=(1,),
        in_specs=[_const_spec((rows, D_MODEL)), _const_spec((1, D_MODEL)),
                  _const_spec((D_MODEL, w)), _const_spec((D_MODEL, w))],
        out_specs=[_const_spec((rows, w)), _const_spec((rows, w))],
        out_shape=[jax.ShapeDtypeStruct((rows, w), BF16)] * 2,
        compiler_params=_cparams(("arbitrary",)),
        name="mem_kv_proj",
    )(mem2, g.reshape(1, D_MODEL), wkv[:, :w].astype(BF16), wkv[:, w:].astype(BF16))


def _cross_kernel(x_ref, g_ref, wq_ref, k_ref, v_ref, wo_ref, out_ref):
    x = x_ref[0]
    n = _rms(x, g_ref[...]).astype(BF16)
    q = (_dot(n, wq_ref[...]) * (X_HEAD_DIM ** -0.5)).astype(BF16)
    heads = []
    for h in range(X_HEADS):
        cols = slice(h * X_HEAD_DIM, (h + 1) * X_HEAD_DIM)
        s = _dot_t(q[:, cols], k_ref[0, :, cols])
        m = jnp.max(s, axis=-1, keepdims=True)
        p = jnp.exp(s - m)
        l = jnp.sum(p, axis=-1, keepdims=True)
        heads.append((_dot(p.astype(BF16), v_ref[0, :, cols]) / l).astype(BF16))
    o = jnp.concatenate(heads, axis=-1)
    out_ref[0] = x + _dot(o, wo_ref[...])


def _cross_attention(x3, g, wq, k, v, wo):
    b, s, _ = x3.shape
    m = k.shape[1]
    w = X_HEADS * X_HEAD_DIM
    return pl.pallas_call(
        _cross_kernel,
        grid=(b, s // ROW_TILE),
        in_specs=[
            pl.BlockSpec((1, ROW_TILE, D_MODEL), lambda bi, i: (bi, i, 0)),
            _const_spec((1, D_MODEL)),
            _const_spec((D_MODEL, w)),
            pl.BlockSpec((1, m, w), lambda bi, i: (bi, 0, 0)),
            pl.BlockSpec((1, m, w), lambda bi, i: (bi, 0, 0)),
            _const_spec((w, D_MODEL)),
        ],
        out_specs=pl.BlockSpec((1, ROW_TILE, D_MODEL), lambda bi, i: (bi, i, 0)),
        out_shape=jax.ShapeDtypeStruct((b, s, D_MODEL), F32),
        compiler_params=_cparams(("parallel", "parallel")),
        name="cross_attn",
    )(x3, g.reshape(1, D_MODEL), wq.astype(BF16), k, v, wo.astype(BF16))


def _mlp_kernel(x_ref, g_ref, w1_ref, w2_ref, gf_ref, out_ref, *, final_norm):
    x = x_ref[...]
    n = _rms(x, g_ref[...]).astype(BF16)
    acc = x
    for c in range(D_FF // FF_CHUNK):
        cols = slice(c * FF_CHUNK, (c + 1) * FF_CHUNK)
        h = jnp.maximum(_dot(n, w1_ref[:, cols]), 0.0)
        acc = acc + _dot((h * h).astype(BF16), w2_ref[cols, :])
    if final_norm:
        acc = _rms(acc, gf_ref[...])
    out_ref[...] = acc


def _mlp(x2, g, w1, w2, gf, final_norm):
    t = x2.shape[0]
    return pl.pallas_call(
        functools.partial(_mlp_kernel, final_norm=final_norm),
        grid=(t // ROW_TILE,),
        in_specs=[
            pl.BlockSpec((ROW_TILE, D_MODEL), lambda i: (i, 0)),
            _const_spec((1, D_MODEL)),
            _const_spec((D_MODEL, D_FF)),
            _const_spec((D_FF, D_MODEL)),
            _const_spec((1, D_MODEL)),
        ],
        out_specs=pl.BlockSpec((ROW_TILE, D_MODEL), lambda i: (i, 0)),
        out_shape=jax.ShapeDtypeStruct((t, D_MODEL), F32),
        compiler_params=_cparams(("parallel",)),
        name="mlp_final" if final_norm else "mlp",
    )(x2, g.reshape(1, D_MODEL), w1.astype(BF16), w2.astype(BF16), gf.reshape(1, D_MODEL))


def _rope_rows(xt, cos, sin):
    lo, half = D_NOPE_DIM, D_ROPE_DIM // 2
    x1, x2 = xt[lo:lo + half], xt[lo + half:lo + 2 * half]
    return jnp.concatenate(
        [xt[:lo], x1 * cos - x2 * sin, x2 * cos + x1 * sin, xt[lo + 2 * half:]], axis=0)


ONES_ROWS = 16


def _with_ones_rows(vt, head_rows):
    ones = jnp.ones((ONES_ROWS, vt.shape[1]), F32)
    parts = []
    for r0 in range(0, vt.shape[0], head_rows):
        parts += [vt[r0:r0 + head_rows], ones]
    return jnp.concatenate(parts, axis=0).astype(BF16)


def _cd_in_kernel(x_ref, pos_ref, g_ref, freq_ref, wqkv_ref, wcq_ref, wckv_ref, wkr_ref,
                  qng_ref, kvng_ref, wuq_ref, wuk_ref, wuv_ref,
                  qct_ref, kc_ref, vct_ref, qdt_ref, kd_ref, vdt_ref):
    n = _rms(x_ref[0], g_ref[...]).astype(BF16)
    qc = _dot(n, wqkv_ref[:, :C_WIDTH])
    qct_ref[0, 0] = (qc * (C_HEAD_DIM ** -0.5 * LOG2E)).T.astype(BF16)
    vct_ref[0, 0] = _with_ones_rows(_dot(n, wqkv_ref[:, 2 * C_WIDTH:]).T, 2 * C_HEAD_DIM)

    ang = freq_ref[...] * pos_ref[0, 0].astype(F32)
    cos, sin = jnp.cos(ang), jnp.sin(ang)

    cq = _rms(_dot(n, wcq_ref[...]), qng_ref[...]).astype(BF16)
    q = _dot(cq, wuq_ref[...])
    ckv = _rms(_dot(n, wckv_ref[...]), kvng_ref[...]).astype(BF16)
    k_nope = _dot(ckv, wuk_ref[...])
    vdt_ref[0, 0] = _with_ones_rows(_dot(ckv, wuv_ref[...]).T, D_V_DIM)
    k_rope = _rope_rows(_dot(n, wkr_ref[...]).T, cos, sin).T
    scale = (D_NOPE_DIM + D_ROPE_DIM) ** -0.5 * LOG2E
    for h in range(D_HEADS):
        cols = slice(h * D_PAD_DIM, (h + 1) * D_PAD_DIM)
        qdt_ref[0, 0, cols, :] = (_rope_rows(q[:, cols].T, cos, sin) * scale).astype(BF16)
        kd_ref[0, :, cols] = (k_nope[:, cols] + k_rope).astype(BF16)
    kc_ref[0] = _dot(n, wqkv_ref[:, C_WIDTH:2 * C_WIDTH]).astype(BF16)


def _pad_heads(w, heads, width):
    k = w.shape[0]
    w = w.reshape(k, heads, width)
    return jnp.pad(w, ((0, 0), (0, 0), (0, D_PAD_DIM - width))).reshape(k, heads * D_PAD_DIM)


def _cd_in(x3, positions, g, w_in, q_norm_g, kv_norm_g, w_uq, w_uk, w_uv):
    b, s, _ = x3.shape
    o3 = 3 * C_WIDTH
    o4 = o3 + D_Q_RANK
    o5 = o4 + D_KV_RANK
    wqkv = w_in[:, :o3].astype(BF16)
    wcq = w_in[:, o3:o4].astype(BF16)
    wckv = w_in[:, o4:o5].astype(BF16)
    wkr = jnp.pad(w_in[:, o5:], ((0, 0), (D_NOPE_DIM, D_PAD_DIM - D_NOPE_DIM - D_ROPE_DIM))).astype(BF16)
    wuq = _pad_heads(w_uq, D_HEADS, D_NOPE_DIM + D_ROPE_DIM).astype(BF16)
    wuk = _pad_heads(w_uk, D_HEADS, D_NOPE_DIM).astype(BF16)
    half = D_ROPE_DIM // 2
    freq = (ROPE_THETA ** (-jnp.arange(half, dtype=F32) / half)).reshape(half, 1)
    dw = D_HEADS * D_PAD_DIM
    vc_rows = C_HEADS * (2 * C_HEAD_DIM + ONES_ROWS)
    vd_rows = D_HEADS * (D_V_DIM + ONES_ROWS)
    nt = s // ROW_TILE
    row = lambda w: pl.BlockSpec((1, ROW_TILE, w), lambda bi, i: (bi, i, 0))
    colmajor = lambda w: pl.BlockSpec((1, 1, w, ROW_TILE), lambda bi, i: (bi, i, 0, 0))
    return pl.pallas_call(
        _cd_in_kernel,
        grid=(b, s // ROW_TILE),
        in_specs=[
            row(D_MODEL), colmajor(1), _const_spec((1, D_MODEL)), _const_spec((half, 1)),
            _const_spec((D_MODEL, o3)), _const_spec((D_MODEL, D_Q_RANK)),
            _const_spec((D_MODEL, D_KV_RANK)), _const_spec((D_MODEL, D_PAD_DIM)),
            _const_spec((1, D_Q_RANK)), _const_spec((1, D_KV_RANK)),
            _const_spec((D_Q_RANK, dw)), _const_spec((D_KV_RANK, dw)),
            _const_spec((D_KV_RANK, D_HEADS * D_V_DIM)),
        ],
        out_specs=[colmajor(C_WIDTH), row(C_WIDTH), colmajor(vc_rows), colmajor(dw), row(dw),
                   colmajor(vd_rows)],
        out_shape=[
            jax.ShapeDtypeStruct((b, nt, C_WIDTH, ROW_TILE), BF16),
            jax.ShapeDtypeStruct((b, s, C_WIDTH), BF16),
            jax.ShapeDtypeStruct((b, nt, vc_rows, ROW_TILE), BF16),
            jax.ShapeDtypeStruct((b, nt, dw, ROW_TILE), BF16),
            jax.ShapeDtypeStruct((b, s, dw), BF16),
            jax.ShapeDtypeStruct((b, nt, vd_rows, ROW_TILE), BF16),
        ],
        compiler_params=_cparams(("parallel", "parallel")),
        name="cd_in_proj",
    )(x3, positions.reshape(b, nt, 1, ROW_TILE), g.reshape(1, D_MODEL), freq, wqkv, wcq, wckv, wkr,
      q_norm_g.reshape(1, D_Q_RANK), kv_norm_g.reshape(1, D_KV_RANK), wuq, wuk,
      w_uv.astype(BF16))


def _causal_kernel(qt_ref, k_ref, vt_ref, *rest, mode, lam_init, n_tiles):
    if mode == "diff":
        lq1_ref, lk1_ref, lq2_ref, lk2_ref, g_ref, out_ref = rest[:6]
    else:
        out_ref = rest[0]
    s_bufs, mx_bufs = rest[-11:-7], rest[-7:-3]
    m_ref, l_ref, acc_ref = rest[-3:]
    t = ATTN_TILE
    v_dims = 2 * C_HEAD_DIM if mode == "diff" else D_V_DIM
    v_rows = v_dims + ONES_ROWS
    n_pairs = n_tiles * (n_tiles + 1) // 2
    assert n_tiles >= 2 and n_pairs % 2 == 0
    feat = lax.broadcasted_iota(jnp.int32, (128, t), 0)

    def score(pair, buf, j):
        qi, kb = pair
        rows = pl.ds(pl.multiple_of(kb * t, t), t)
        if mode == "diff":
            qt = qt_ref[0, qi]
            mine = (feat < C_HEAD_DIM) if j == 0 else (feat >= C_HEAD_DIM)
            q, k = jnp.where(mine, qt, jnp.zeros_like(qt)), k_ref[0, rows, :]
        else:
            cols = slice(j * D_PAD_DIM, (j + 1) * D_PAD_DIM)
            q, k = qt_ref[0, qi, cols, :], k_ref[0, rows, cols]
        s = _dot(k, q)
        s_bufs[buf][j] = s
        mx_bufs[buf][j] = jnp.max(s, axis=0, keepdims=True)

    def finish_tile(qi):
        a1 = acc_ref[0] / l_ref[0]
        a2 = acc_ref[1] / l_ref[1]
        if mode == "diff":
            lam = (jnp.exp(jnp.sum(lq1_ref[...] * lk1_ref[...], axis=-1, keepdims=True))
                   - jnp.exp(jnp.sum(lq2_ref[...] * lk2_ref[...], axis=-1, keepdims=True))
                   + lam_init)
            d = a1 - lam * a2
            out = (d * lax.rsqrt(jnp.mean(d * d, axis=0, keepdims=True) + EPS) * g_ref[...]
                   * (1.0 - lam_init))
        else:
            out = jnp.concatenate([a1, a2], axis=0)
        out_ref[0, pl.ds(pl.multiple_of(qi * t, t), t), :] = out.T.astype(BF16)

    def consume(pair, buf, j, diagonal):
        qi, kb = pair
        first = kb == 0
        s = s_bufs[buf][j]
        if diagonal:
            kr = lax.broadcasted_iota(jnp.int32, (t, t), 0)
            qc = lax.broadcasted_iota(jnp.int32, (t, t), 1)
            s = jnp.where(kr <= qc, s, NEG)
            block_max = jnp.max(s, axis=0, keepdims=True)
        else:
            block_max = mx_bufs[buf][j]
        m = jnp.where(first, NEG, m_ref[j])
        l = jnp.where(first, 0.0, l_ref[j])
        m_new = jnp.maximum(m, block_max)
        alpha = jnp.exp2(m - m_new)
        p = jnp.exp2((s - m_new).astype(BF16))
        m_ref[j] = m_new
        if mode == "diff":
            vt = vt_ref[0, kb]
        else:
            vt = vt_ref[0, kb, j * v_rows:(j + 1) * v_rows, :]
        pv = _dot(vt, p)
        l_ref[j] = alpha * l + pv[v_dims:v_dims + 1]
        acc_ref[j] = alpha * acc_ref[j] + pv[:v_dims]
        if diagonal and j == 1:
            finish_tile(qi)

    def advance(pair):
        qi, kb = pair
        end = kb == qi
        nqi, nkb = jnp.where(end, qi + 1, qi), jnp.where(end, 0, kb + 1)
        done = nqi >= n_tiles
        return jnp.where(done, qi, nqi), jnp.where(done, kb, nkb)

    acc_ref[...] = jnp.zeros(acc_ref.shape, F32)
    pair0 = (jnp.int32(0), jnp.int32(0))
    pair1 = advance(pair0)
    for j in range(2):
        score(pair0, 0, j)
        score(pair1, 1, j)

    def body(trip, carry):
        pa, pb = carry[:2], carry[2:]
        pc = advance(pb)
        pd = advance(pc)
        diag_a, diag_b = pa[0] == pa[1], pb[0] == pb[1]
        for half in range(2):
            ra, rb, wc, wd = 2 * half, 2 * half + 1, 2 - 2 * half, 3 - 2 * half
            for da, db, cond in ((False, False, jnp.logical_not(diag_a | diag_b)),
                                 (True, False, diag_a), (False, True, diag_b)):
                @pl.when(cond & (trip % 2 == half))
                def _():
                    score(pc, wc, 0)
                    score(pc, wc, 1)
                    consume(pa, ra, 0, da)
                    score(pd, wd, 0)
                    consume(pa, ra, 1, da)
                    consume(pb, rb, 0, db)
                    score(pd, wd, 1)
                    consume(pb, rb, 1, db)
        return (*pc, *pd)

    lax.fori_loop(0, n_pairs // 2, body, (*pair0, *pair1))


def _causal_attention(qt, k, vt, mode, extras=(), lam_init=0.0):
    b, s, _ = k.shape
    t = ATTN_TILE
    assert qt.shape[3] == t and vt.shape[3] == t
    qw = 128 if mode == "diff" else 2 * D_PAD_DIM
    groups = k.shape[2] // qw
    in_specs = [
        pl.BlockSpec((1, s // t, qw, t), lambda bi, h: (bi, 0, h, 0)),
        pl.BlockSpec((1, s, qw), lambda bi, h: (bi, 0, h)),
        pl.BlockSpec((1, s // t, vt.shape[2] // groups, t), lambda bi, h: (bi, 0, h, 0)),
    ] + [_const_spec(e.shape) for e in extras]
    stats = [pltpu.VMEM((2, 1, t), F32)] * 2
    return pl.pallas_call(
        functools.partial(_causal_kernel, mode=mode, lam_init=lam_init, n_tiles=s // t),
        grid=(b, groups),
        in_specs=in_specs,
        out_specs=pl.BlockSpec((1, s, 128), lambda bi, h: (bi, 0, h)),
        out_shape=jax.ShapeDtypeStruct((b, s, groups * 128), BF16),
        scratch_shapes=[
            *[pltpu.VMEM((2, t, t), F32)] * 4,
            *stats, *stats,
            *stats,
            pltpu.VMEM((2, 128 if mode == "diff" else D_V_DIM, t), F32),
        ],
        compiler_params=_cparams(("parallel", "parallel")),
        name=f"causal_attn_{mode}",
    )(qt, k, vt, *extras)


def _cd_out_kernel(x_ref, yc_ref, yd_ref, wc_ref, wd_ref, out_ref):
    out_ref[...] = x_ref[...] + _dot(yc_ref[...], wc_ref[...]) + _dot(yd_ref[...], wd_ref[...])


def _cd_out(x2, yc, yd, w_out):
    t = x2.shape[0]
    row = lambda w: pl.BlockSpec((ROW_TILE, w), lambda i: (i, 0))
    dvw = D_HEADS * D_V_DIM
    return pl.pallas_call(
        _cd_out_kernel,
        grid=(t // ROW_TILE,),
        in_specs=[row(D_MODEL), row(C_WIDTH), row(dvw),
                  _const_spec((C_WIDTH, D_MODEL)), _const_spec((dvw, D_MODEL))],
        out_specs=row(D_MODEL),
        out_shape=jax.ShapeDtypeStruct((t, D_MODEL), F32),
        compiler_params=_cparams(("parallel",)),
        name="cd_out_proj",
    )(x2, yc, yd, w_out[:C_WIDTH].astype(BF16), w_out[C_WIDTH:].astype(BF16))


def kernel(x, mem, positions, norm_mix_g, norm_cross_g, norm_mem_g, cross_wq, cross_wkv, cross_wo,
           norm_mlp_g, mlp_w1, mlp_w2, ab_w_in, ab_w_out, ab_conv_w, ab_conv_b, ab_ln_g, ab_ln_b,
           cd_w_in, cd_w_out, diff_lq1, diff_lk1, diff_lq2, diff_lk2, diff_subln_g, mla_q_norm_g,
           mla_kv_norm_g, mla_w_uq, mla_w_uk, mla_w_uv, final_norm_g):
    b, s, d = x.shape
    t = b * s
    mem_len = mem.shape[1]
    depth = norm_mix_g.shape[0]
    xw = X_HEADS * X_HEAD_DIM
    x2 = x.reshape(t, d)
    for i in range(depth):
        j = i // 2
        if i % 2 == 0:
            *views, glu = _ab_in(x2.reshape(b, s, d), norm_mix_g[i], ab_w_in[j])
            outs, lses = zip(*[_dilated_pattern(view, dil)
                               for view, (_, dil) in zip(views, A_PATTERNS)])
            yb = _conv_module(glu, ab_conv_w[j], ab_conv_b[j], ab_ln_g[j], ab_ln_b[j])
            x2 = _ab_out(x2.reshape(b, s, d), outs, lses, yb, ab_w_out[j]).reshape(t, d)
        else:
            qc, kc, vc, qd, kd, vd = _cd_in(
                x2.reshape(b, s, d), positions, norm_mix_g[i], cd_w_in[j], mla_q_norm_g[j],
                mla_kv_norm_g[j], mla_w_uq[j], mla_w_uk[j], mla_w_uv[j])
            lam_init = 0.8 - 0.6 * math.exp(-0.3 * i)
            vec = lambda a: a.reshape(1, -1)
            yc = _causal_attention(
                qc, kc, vc, "diff",
                extras=(vec(diff_lq1[j]), vec(diff_lk1[j]), vec(diff_lq2[j]), vec(diff_lk2[j]),
                        diff_subln_g[j].reshape(-1, 1)),
                lam_init=lam_init)
            yd = _causal_attention(qd, kd, vd, "mla")
            x2 = _cd_out(x2, yc.reshape(t, C_WIDTH), yd.reshape(t, D_HEADS * D_V_DIM), cd_w_out[j])
        km, vm = _mem_kv(mem.reshape(b * mem_len, d), norm_mem_g[i], cross_wkv[i])
        x2 = _cross_attention(x2.reshape(b, s, d), norm_cross_g[i], cross_wq[i],
                              km.reshape(b, mem_len, xw), vm.reshape(b, mem_len, xw),
                              cross_wo[i]).reshape(t, d)
        x2 = _mlp(x2, norm_mlp_g[i], mlp_w1[i], mlp_w2[i], final_norm_g, i == depth - 1)
    return x2.reshape(b, s, d)
```

```python
import functools
import math

import jax
import jax.numpy as jnp
from jax import lax
from jax.experimental import pallas as pl
from jax.experimental.pallas import tpu as pltpu

F32 = jnp.float32
BF16 = jnp.bfloat16

D_MODEL = 1024
EPS = 1e-6
LOG2E = math.log2(math.e)
NEG = -1e30

A_HEADS = 8
A_HEAD_DIM = 64
A_WIDTH = A_HEADS * A_HEAD_DIM
A_PATTERNS = ((128, 1), (512, 4), (2048, 16))
A_BLK = 128
assert all(window // dilation == A_BLK for window, dilation in A_PATTERNS)
B_CHANNELS = 512
B_CONV_WIDTH = 31
B_HALO = 32
C_HEADS = 4
C_HEAD_DIM = 64
C_WIDTH = C_HEADS * 2 * C_HEAD_DIM
D_HEADS = 8
D_NOPE_DIM = 64
D_ROPE_DIM = 32
D_V_DIM = 64
D_Q_RANK = 384
D_KV_RANK = 256
D_PAD_DIM = 128
ROPE_THETA = 10000.0
X_HEADS = 4
X_HEAD_DIM = 128
D_FF = 4 * D_MODEL
FF_CHUNK = 1024

ROW_TILE = 512
ATTN_TILE = ROW_TILE
VMEM_LIMIT = 56 * 1024 * 1024


def _cparams(sem):
    return pltpu.CompilerParams(dimension_semantics=sem, vmem_limit_bytes=VMEM_LIMIT)


def _rms(x, g):
    return x * lax.rsqrt(jnp.mean(x * x, axis=-1, keepdims=True) + EPS) * g


def _dot(a, b):
    return jnp.dot(a, b, preferred_element_type=F32)


def _dot_t(a, b):
    return lax.dot_general(a, b, (((1,), (1,)), ((), ())), preferred_element_type=F32)


def _const_spec(shape):
    nd = len(shape)
    return pl.BlockSpec(shape, lambda *_: (0,) * nd)


def _ab_in_kernel(x_ref, g_ref, wqkv_ref, wu_ref, wg_ref, *rest):
    views, glu_ref, qkv_scr = rest[:-2], rest[-2], rest[-1]
    n = _rms(x_ref[0], g_ref[...]).astype(BF16)
    width = 3 * A_WIDTH
    dense = [v for (_, d), v in zip(A_PATTERNS, views) if d == 1]
    chunk = 256
    assert A_WIDTH % chunk == 0
    for c0 in range(0, width, chunk):
        part = _dot(n, wqkv_ref[:, c0:c0 + chunk])
        if c0 < A_WIDTH:
            part = part * (A_HEAD_DIM ** -0.5 * LOG2E)
        for c in range(c0 // 128, (c0 + chunk) // 128):
            qkv_scr[c] = part[:, c * 128 - c0:(c + 1) * 128 - c0]
        for view_ref in dense:
            view_ref[0, :, c0:c0 + chunk] = part.astype(BF16)
    for (_, d), view_ref in zip(A_PATTERNS, views):
        if d == 1:
            continue
        for r in range(d):
            for c in range(width // 128):
                view_ref[0, :, r * width + c * 128:r * width + (c + 1) * 128] = (
                    qkv_scr[c, pl.ds(r, ROW_TILE // d, stride=d), :].astype(BF16))
    u = _dot(n, wu_ref[...])
    gate = _dot(n, wg_ref[...])
    glu_ref[0] = u * jax.nn.sigmoid(gate)


def _ab_in(x3, g, w_in):
    b, s, _ = x3.shape
    width = 3 * A_WIDTH
    wqkv = w_in[:, :width].astype(BF16)
    wu = w_in[:, width:width + B_CHANNELS].astype(BF16)
    wg = w_in[:, width + B_CHANNELS:].astype(BF16)
    row = lambda rows, w: pl.BlockSpec((1, rows, w), lambda bi, i: (bi, i, 0))
    dils = [d for _, d in A_PATTERNS]
    return pl.pallas_call(
        _ab_in_kernel,
        grid=(b, s // ROW_TILE),
        in_specs=[
            row(ROW_TILE, D_MODEL),
            _const_spec((1, D_MODEL)),
            _const_spec((D_MODEL, width)),
            _const_spec((D_MODEL, B_CHANNELS)),
            _const_spec((D_MODEL, B_CHANNELS)),
        ],
        out_specs=[row(ROW_TILE // d, d * width) for d in dils] + [row(ROW_TILE, B_CHANNELS)],
        out_shape=[jax.ShapeDtypeStruct((b, s // d, d * width), BF16) for d in dils]
        + [jax.ShapeDtypeStruct((b, s, B_CHANNELS), F32)],
        scratch_shapes=[pltpu.VMEM((width // 128, ROW_TILE, 128), F32)],
        compiler_params=_cparams(("parallel", "parallel")),
        name="ab_in_proj",
    )(x3, g.reshape(1, D_MODEL), wqkv, wu, wg)


def _dilated_kernel(q_ref, kp_ref, kc_ref, vp_ref, vc_ref, o_ref, lse_ref, s_ref, *, rows):
    first_tile = pl.program_id(2) == 0
    qi = lax.broadcasted_iota(jnp.int32, (A_BLK, 2 * A_BLK), 0)
    kj = lax.broadcasted_iota(jnp.int32, (A_BLK, 2 * A_BLK), 1)
    band = ((kj < A_BLK) & (kj >= qi)) | ((kj >= A_BLK) & (kj - A_BLK <= qi))
    lane = lax.broadcasted_iota(jnp.int32, (A_BLK, 128), 1)
    low = lane < A_HEAD_DIM
    pairs = A_WIDTH // 128
    nblk = rows // A_BLK

    def blk(i):
        return slice(i * A_BLK, (i + 1) * A_BLK)

    def prev_and_cur(i, prev_ref, cur_ref, cols):
        prev = prev_ref[0, :, cols] if i == 0 else cur_ref[0, blk(i - 1), cols]
        return jnp.concatenate([prev, cur_ref[0, blk(i), cols]], axis=0)

    def scores(i):
        for pr in range(pairs):
            cols = slice(pr * 128, (pr + 1) * 128)
            q = q_ref[0, blk(i), cols]
            k = prev_and_cur(i, kp_ref, kc_ref, cols)
            zero = jnp.zeros_like(q)
            s_ref[i % 2, 2 * pr] = _dot_t(jnp.where(low, q, zero), k)
            s_ref[i % 2, 2 * pr + 1] = _dot_t(jnp.where(low, zero, q), k)

    def consume(i):
        valid = band & ((kj >= A_BLK) | jnp.logical_not(first_tile)) if i == 0 else band
        for pr in range(pairs):
            cols = slice(pr * 128, (pr + 1) * 128)
            v = prev_and_cur(i, vp_ref, vc_ref, cols)
            outs, lses = [], []
            for half in range(2):
                s = jnp.where(valid, s_ref[i % 2, 2 * pr + half], NEG)
                m = jnp.max(s, axis=-1, keepdims=True)
                p = jnp.exp2(s - m)
                l = jnp.sum(p, axis=-1, keepdims=True)
                outs.append(_dot(p.astype(BF16), v) / l)
                lses.append(m + jnp.log2(l))
            o_ref[0, blk(i), cols] = jnp.where(low, outs[0], outs[1]).astype(BF16)
            lse_ref[0, blk(i), cols] = jnp.where(low, lses[0], lses[1])

    scores(0)
    for i in range(nblk):
        if i + 1 < nblk:
            scores(i + 1)
        consume(i)


def _dilated_pattern(view, dilation):
    b, sub, _ = view.shape
    rows = min(sub, 512)
    assert sub % rows == 0 and rows % A_BLK == 0
    blocks_per_tile = rows // A_BLK

    def cur_spec(part):
        return pl.BlockSpec((1, rows, A_WIDTH), lambda bi, r, n: (bi, n, 3 * r + part))

    def prev_spec(part):
        return pl.BlockSpec(
            (1, A_BLK, A_WIDTH),
            lambda bi, r, n: (bi, jnp.maximum(n * blocks_per_tile - 1, 0), 3 * r + part))

    out_spec = pl.BlockSpec((1, rows, A_WIDTH), lambda bi, r, n: (bi, n, r))
    return pl.pallas_call(
        functools.partial(_dilated_kernel, rows=rows),
        grid=(b, dilation, sub // rows),
        in_specs=[cur_spec(0), prev_spec(1), cur_spec(1), prev_spec(2), cur_spec(2)],
        out_specs=[out_spec, out_spec],
        out_shape=[
            jax.ShapeDtypeStruct((b, sub, dilation * A_WIDTH), BF16),
            jax.ShapeDtypeStruct((b, sub, dilation * A_WIDTH), F32),
        ],
        scratch_shapes=[pltpu.VMEM((2, A_HEADS, A_BLK, 2 * A_BLK), F32)],
        compiler_params=_cparams(("parallel", "parallel", "arbitrary")),
        name=f"dilated_attn_d{dilation}",
    )(view, view, view, view, view)


CONV_ROWS = 128
SUBLANES = 8


def _conv_kernel(halo_ref, cur_ref, w_ref, cb_ref, g_ref, b_ref, out_ref, buf_ref, *, rows):
    first_tile = pl.program_id(1) == 0
    halo = halo_ref[0]
    buf_ref[0:B_HALO, :] = jnp.where(first_tile, jnp.zeros_like(halo), halo)
    buf_ref[B_HALO:B_HALO + rows, :] = cur_ref[0]
    lead = B_HALO - (B_CONV_WIDTH - 1)

    def chunk(c, carry):
        base = pl.multiple_of(c * CONV_ROWS, CONV_ROWS)
        span = CONV_ROWS + B_HALO
        accs = []
        for c in range(B_CHANNELS // 128):
            cols = slice(c * 128, (c + 1) * 128)
            win = buf_ref[pl.ds(base, span), cols]
            acc = jnp.zeros((CONV_ROWS, 128), F32)
            for r in range(SUBLANES):
                shifted = pltpu.roll(win, span - r, 0) if r else win
                for tap in range(B_CONV_WIDTH):
                    if (lead + tap) % SUBLANES == r:
                        a = (lead + tap) // SUBLANES * SUBLANES
                        acc = acc + shifted[a:a + CONV_ROWS, :] * w_ref[tap:tap + 1, cols]
            accs.append(acc)
        y = jnp.concatenate(accs, axis=-1) + cb_ref[...]
        mu = jnp.mean(y, axis=-1, keepdims=True)
        yc = y - mu
        var = jnp.mean(yc * yc, axis=-1, keepdims=True)
        z = yc * lax.rsqrt(var + EPS) * g_ref[...] + b_ref[...]
        out_ref[0, pl.ds(base, CONV_ROWS), :] = (z * jax.nn.sigmoid(z)).astype(BF16)
        return carry

    lax.fori_loop(0, rows // CONV_ROWS, chunk, 0)


def _conv_module(glu, conv_w, conv_b, ln_g, ln_b):
    b, s, c = glu.shape
    rows = ROW_TILE
    halo_blocks = rows // B_HALO
    vec = lambda a: a.reshape(1, c)
    return pl.pallas_call(
        functools.partial(_conv_kernel, rows=rows),
        grid=(b, s // rows),
        in_specs=[
            pl.BlockSpec((1, B_HALO, c), lambda bi, n: (bi, jnp.maximum(n * halo_blocks - 1, 0), 0)),
            pl.BlockSpec((1, rows, c), lambda bi, n: (bi, n, 0)),
            _const_spec((B_CONV_WIDTH, c)),
            _const_spec((1, c)), _const_spec((1, c)), _const_spec((1, c)),
        ],
        out_specs=pl.BlockSpec((1, rows, c), lambda bi, n: (bi, n, 0)),
        out_shape=jax.ShapeDtypeStruct((b, s, c), BF16),
        scratch_shapes=[pltpu.VMEM((B_HALO + rows, c), F32)],
        compiler_params=_cparams(("parallel", "arbitrary")),
        name="conformer_conv",
    )(glu, glu, conv_w.reshape(B_CONV_WIDTH, c), vec(conv_b), vec(ln_g), vec(ln_b))


def _ab_out_kernel(x_ref, *rest):
    n_pat = len(A_PATTERNS)
    o_refs, l_refs = rest[:n_pat], rest[n_pat:2 * n_pat]
    yb_ref, wa_ref, wb_ref, out_ref = rest[2 * n_pat:2 * n_pat + 4]
    scr = list(rest[2 * n_pat + 4:])
    tiles = A_WIDTH // 128
    gather = lambda ref: jnp.concatenate([ref[c] for c in range(tiles)], axis=-1)
    outs, lses = [], []
    for idx, (_, d) in enumerate(A_PATTERNS):
        if d == 1:
            outs.append(o_refs[idx][0].astype(F32))
            lses.append(l_refs[idx][0])
            continue
        o_scr, l_scr = scr.pop(0), scr.pop(0)
        for r in range(d):
            dst = pl.ds(r, ROW_TILE // d, stride=d)
            for c in range(tiles):
                cols = slice(r * A_WIDTH + c * 128, r * A_WIDTH + (c + 1) * 128)
                o_scr[c, dst, :] = o_refs[idx][0, :, cols].astype(F32)
                l_scr[c, dst, :] = l_refs[idx][0, :, cols]
        outs.append(gather(o_scr))
        lses.append(gather(l_scr))
    m = functools.reduce(jnp.maximum, lses)
    es = [jnp.exp2(l - m) for l in lses]
    ya = sum(e * o for e, o in zip(es, outs)) / sum(es)
    out_ref[0] = (x_ref[0] + _dot(ya.astype(BF16), wa_ref[...])
                  + _dot(yb_ref[0], wb_ref[...]))


def _ab_out(x3, outs, lses, yb, w_out):
    b, s, _ = x3.shape
    row = lambda rows, w: pl.BlockSpec((1, rows, w), lambda bi, i: (bi, i, 0))
    views = [row(ROW_TILE // d, d * A_WIDTH) for _, d in A_PATTERNS]
    scratch = []
    for _, d in A_PATTERNS:
        if d > 1:
            scratch += [pltpu.VMEM((A_WIDTH // 128, ROW_TILE, 128), F32)] * 2
    return pl.pallas_call(
        _ab_out_kernel,
        grid=(b, s // ROW_TILE),
        in_specs=[row(ROW_TILE, D_MODEL)] + views + views + [
            row(ROW_TILE, B_CHANNELS),
            _const_spec((A_WIDTH, D_MODEL)), _const_spec((B_CHANNELS, D_MODEL))],
        out_specs=row(ROW_TILE, D_MODEL),
        out_shape=jax.ShapeDtypeStruct((b, s, D_MODEL), F32),
        scratch_shapes=scratch,
        compiler_params=_cparams(("parallel", "parallel")),
        name="ab_out_proj",
    )(x3, *outs, *lses, yb, w_out[:A_WIDTH].astype(BF16), w_out[A_WIDTH:].astype(BF16))


def _mem_kv_kernel(mem_ref, g_ref, wk_ref, wv_ref, k_ref, v_ref):
    n = _rms(mem_ref[...], g_ref[...]).astype(BF16)
    k_ref[...] = _dot(n, wk_ref[...]).astype(BF16)
    v_ref[...] = _dot(n, wv_ref[...]).astype(BF16)


def _mem_kv(mem2, g, wkv):
    rows = mem2.shape[0]
    w = X_HEADS * X_HEAD_DIM
    return pl.pallas_call(
        _mem_kv_kernel,
        grid=(1,),
        in_specs=[_const_spec((rows, D_MODEL)), _const_spec((1, D_MODEL)),
                  _const_spec((D_MODEL, w)), _const_spec((D_MODEL, w))],
        out_specs=[_const_spec((rows, w)), _const_spec((rows, w))],
        out_shape=[jax.ShapeDtypeStruct((rows, w), BF16)] * 2,
        compiler_params=_cparams(("arbitrary",)),
        name="mem_kv_proj",
    )(mem2, g.reshape(1, D_MODEL), wkv[:, :w].astype(BF16), wkv[:, w:].astype(BF16))


def _cross_kernel(x_ref, g_ref, wq_ref, k_ref, v_ref, wo_ref, out_ref):
    x = x_ref[0]
    n = _rms(x, g_ref[...]).astype(BF16)
    q = (_dot(n, wq_ref[...]) * (X_HEAD_DIM ** -0.5)).astype(BF16)
    heads = []
    for h in range(X_HEADS):
        cols = slice(h * X_HEAD_DIM, (h + 1) * X_HEAD_DIM)
        s = _dot_t(q[:, cols], k_ref[0, :, cols])
        m = jnp.max(s, axis=-1, keepdims=True)
        p = jnp.exp(s - m)
        l = jnp.sum(p, axis=-1, keepdims=True)
        heads.append((_dot(p.astype(BF16), v_ref[0, :, cols]) / l).astype(BF16))
    o = jnp.concatenate(heads, axis=-1)
    out_ref[0] = x + _dot(o, wo_ref[...])


def _cross_attention(x3, g, wq, k, v, wo):
    b, s, _ = x3.shape
    m = k.shape[1]
    w = X_HEADS * X_HEAD_DIM
    return pl.pallas_call(
        _cross_kernel,
        grid=(b, s // ROW_TILE),
        in_specs=[
            pl.BlockSpec((1, ROW_TILE, D_MODEL), lambda bi, i: (bi, i, 0)),
            _const_spec((1, D_MODEL)),
            _const_spec((D_MODEL, w)),
            pl.BlockSpec((1, m, w), lambda bi, i: (bi, 0, 0)),
            pl.BlockSpec((1, m, w), lambda bi, i: (bi, 0, 0)),
            _const_spec((w, D_MODEL)),
        ],
        out_specs=pl.BlockSpec((1, ROW_TILE, D_MODEL), lambda bi, i: (bi, i, 0)),
        out_shape=jax.ShapeDtypeStruct((b, s, D_MODEL), F32),
        compiler_params=_cparams(("parallel", "parallel")),
        name="cross_attn",
    )(x3, g.reshape(1, D_MODEL), wq.astype(BF16), k, v, wo.astype(BF16))


def _mlp_kernel(x_ref, g_ref, w1_ref, w2_ref, gf_ref, out_ref, *, final_norm):
    x = x_ref[...]
    n = _rms(x, g_ref[...]).astype(BF16)
    acc = x
    for c in range(D_FF // FF_CHUNK):
        cols = slice(c * FF_CHUNK, (c + 1) * FF_CHUNK)
        h = jnp.maximum(_dot(n, w1_ref[:, cols]), 0.0)
        acc = acc + _dot((h * h).astype(BF16), w2_ref[cols, :])
    if final_norm:
        acc = _rms(acc, gf_ref[...])
    out_ref[...] = acc


def _mlp(x2, g, w1, w2, gf, final_norm):
    t = x2.shape[0]
    return pl.pallas_call(
        functools.partial(_mlp_kernel, final_norm=final_norm),
        grid=(t // ROW_TILE,),
        in_specs=[
            pl.BlockSpec((ROW_TILE, D_MODEL), lambda i: (i, 0)),
            _const_spec((1, D_MODEL)),
            _const_spec((D_MODEL, D_FF)),
            _const_spec((D_FF, D_MODEL)),
            _const_spec((1, D_MODEL)),
        ],
        out_specs=pl.BlockSpec((ROW_TILE, D_MODEL), lambda i: (i, 0)),
        out_shape=jax.ShapeDtypeStruct((t, D_MODEL), F32),
        compiler_params=_cparams(("parallel",)),
        name="mlp_final" if final_norm else "mlp",
    )(x2, g.reshape(1, D_MODEL), w1.astype(BF16), w2.astype(BF16), gf.reshape(1, D_MODEL))


def _rope_rows(xt, cos, sin):
    lo, half = D_NOPE_DIM, D_ROPE_DIM // 2
    x1, x2 = xt[lo:lo + half], xt[lo + half:lo + 2 * half]
    return jnp.concatenate(
        [xt[:lo], x1 * cos - x2 * sin, x2 * cos + x1 * sin, xt[lo + 2 * half:]], axis=0)


ONES_ROWS = 16


def _with_ones_rows(vt, head_rows):
    ones = jnp.ones((ONES_ROWS, vt.shape[1]), F32)
    parts = []
    for r0 in range(0, vt.shape[0], head_rows):
        parts += [vt[r0:r0 + head_rows], ones]
    return jnp.concatenate(parts, axis=0).astype(BF16)


def _cd_in_kernel(x_ref, pos_ref, g_ref, freq_ref, wqkv_ref, wcq_ref, wckv_ref, wkr_ref,
                  qng_ref, kvng_ref, wuq_ref, wuk_ref, wuv_ref,
                  qct_ref, kc_ref, vct_ref, qdt_ref, kd_ref, vdt_ref):
    n = _rms(x_ref[0], g_ref[...]).astype(BF16)
    qc = _dot(n, wqkv_ref[:, :C_WIDTH])
    qct_ref[0, 0] = (qc * (C_HEAD_DIM ** -0.5 * LOG2E)).T.astype(BF16)
    vct_ref[0, 0] = _with_ones_rows(_dot(n, wqkv_ref[:, 2 * C_WIDTH:]).T, 2 * C_HEAD_DIM)

    ang = freq_ref[...] * pos_ref[0, 0].astype(F32)
    cos, sin = jnp.cos(ang), jnp.sin(ang)

    cq = _rms(_dot(n, wcq_ref[...]), qng_ref[...]).astype(BF16)
    q = _dot(cq, wuq_ref[...])
    ckv = _rms(_dot(n, wckv_ref[...]), kvng_ref[...]).astype(BF16)
    k_nope = _dot(ckv, wuk_ref[...])
    vdt_ref[0, 0] = _with_ones_rows(_dot(ckv, wuv_ref[...]).T, D_V_DIM)
    k_rope = _rope_rows(_dot(n, wkr_ref[...]).T, cos, sin).T
    scale = (D_NOPE_DIM + D_ROPE_DIM) ** -0.5 * LOG2E
    for h in range(D_HEADS):
        cols = slice(h * D_PAD_DIM, (h + 1) * D_PAD_DIM)
        qdt_ref[0, 0, cols, :] = (_rope_rows(q[:, cols].T, cos, sin) * scale).astype(BF16)
        kd_ref[0, :, cols] = (k_nope[:, cols] + k_rope).astype(BF16)
    kc_ref[0] = _dot(n, wqkv_ref[:, C_WIDTH:2 * C_WIDTH]).astype(BF16)


def _pad_heads(w, heads, width):
    k = w.shape[0]
    w = w.reshape(k, heads, width)
    return jnp.pad(w, ((0, 0), (0, 0), (0, D_PAD_DIM - width))).reshape(k, heads * D_PAD_DIM)


def _cd_in(x3, positions, g, w_in, q_norm_g, kv_norm_g, w_uq, w_uk, w_uv):
    b, s, _ = x3.shape
    o3 = 3 * C_WIDTH
    o4 = o3 + D_Q_RANK
    o5 = o4 + D_KV_RANK
    wqkv = w_in[:, :o3].astype(BF16)
    wcq = w_in[:, o3:o4].astype(BF16)
    wckv = w_in[:, o4:o5].astype(BF16)
    wkr = jnp.pad(w_in[:, o5:], ((0, 0), (D_NOPE_DIM, D_PAD_DIM - D_NOPE_DIM - D_ROPE_DIM))).astype(BF16)
    wuq = _pad_heads(w_uq, D_HEADS, D_NOPE_DIM + D_ROPE_DIM).astype(BF16)
    wuk = _pad_heads(w_uk, D_HEADS, D_NOPE_DIM).astype(BF16)
    half = D_ROPE_DIM // 2
    freq = (ROPE_THETA ** (-jnp.arange(half, dtype=F32) / half)).reshape(half, 1)
    dw = D_HEADS * D_PAD_DIM
    vc_rows = C_HEADS * (2 * C_HEAD_DIM + ONES_ROWS)
    vd_rows = D_HEADS * (D_V_DIM + ONES_ROWS)
    nt = s // ROW_TILE
    row = lambda w: pl.BlockSpec((1, ROW_TILE, w), lambda bi, i: (bi, i, 0))
    colmajor = lambda w: pl.BlockSpec((1, 1, w, ROW_TILE), lambda bi, i: (bi, i, 0, 0))
    return pl.pallas_call(
        _cd_in_kernel,
        grid=(b, s // ROW_TILE),
        in_specs=[
            row(D_MODEL), colmajor(1), _const_spec((1, D_MODEL)), _const_spec((half, 1)),
            _const_spec((D_MODEL, o3)), _const_spec((D_MODEL, D_Q_RANK)),
            _const_spec((D_MODEL, D_KV_RANK)), _const_spec((D_MODEL, D_PAD_DIM)),
            _const_spec((1, D_Q_RANK)), _const_spec((1, D_KV_RANK)),
            _const_spec((D_Q_RANK, dw)), _const_spec((D_KV_RANK, dw)),
            _const_spec((D_KV_RANK, D_HEADS * D_V_DIM)),
        ],
        out_specs=[colmajor(C_WIDTH), row(C_WIDTH), colmajor(vc_rows), colmajor(dw), row(dw),
                   colmajor(vd_rows)],
        out_shape=[
            jax.ShapeDtypeStruct((b, nt, C_WIDTH, ROW_TILE), BF16),
            jax.ShapeDtypeStruct((b, s, C_WIDTH), BF16),
            jax.ShapeDtypeStruct((b, nt, vc_rows, ROW_TILE), BF16),
            jax.ShapeDtypeStruct((b, nt, dw, ROW_TILE), BF16),
            jax.ShapeDtypeStruct((b, s, dw), BF16),
            jax.ShapeDtypeStruct((b, nt, vd_rows, ROW_TILE), BF16),
        ],
        compiler_params=_cparams(("parallel", "parallel")),
        name="cd_in_proj",
    )(x3, positions.reshape(b, nt, 1, ROW_TILE), g.reshape(1, D_MODEL), freq, wqkv, wcq, wckv, wkr,
      q_norm_g.reshape(1, D_Q_RANK), kv_norm_g.reshape(1, D_KV_RANK), wuq, wuk,
      w_uv.astype(BF16))


def _causal_kernel(qt_ref, k_ref, vt_ref, *rest, mode, lam_init, n_tiles):
    if mode == "diff":
        lq1_ref, lk1_ref, lq2_ref, lk2_ref, g_ref, out_ref = rest[:6]
    else:
        out_ref = rest[0]
    s_bufs, mx_bufs = rest[-11:-7], rest[-7:-3]
    m_ref, l_ref, acc_ref = rest[-3:]
    t = ATTN_TILE
    v_dims = 2 * C_HEAD_DIM if mode == "diff" else D_V_DIM
    v_rows = v_dims + ONES_ROWS
    n_pairs = n_tiles * (n_tiles + 1) // 2
    assert n_tiles >= 2 and n_pairs % 2 == 0
    feat = lax.broadcasted_iota(jnp.int32, (128, t), 0)

    def score(pair, buf, j):
        qi, kb = pair
        rows = pl.ds(pl.multiple_of(kb * t, t), t)
        if mode == "diff":
            qt = qt_ref[0, qi]
            mine = (feat < C_HEAD_DIM) if j == 0 else (feat >= C_HEAD_DIM)
            q, k = jnp.where(mine, qt, jnp.zeros_like(qt)), k_ref[0, rows, :]
        else:
            cols = slice(j * D_PAD_DIM, (j + 1) * D_PAD_DIM)
            q, k = qt_ref[0, qi, cols, :], k_ref[0, rows, cols]
        s = _dot(k, q)
        s_bufs[buf][j] = s
        mx_bufs[buf][j] = jnp.max(s, axis=0, keepdims=True)

    def finish_tile(qi):
        a1 = acc_ref[0] / l_ref[0]
        a2 = acc_ref[1] / l_ref[1]
        if mode == "diff":
            lam = (jnp.exp(jnp.sum(lq1_ref[...] * lk1_ref[...], axis=-1, keepdims=True))
                   - jnp.exp(jnp.sum(lq2_ref[...] * lk2_ref[...], axis=-1, keepdims=True))
                   + lam_init)
            d = a1 - lam * a2
            out = (d * lax.rsqrt(jnp.mean(d * d, axis=0, keepdims=True) + EPS) * g_ref[...]
                   * (1.0 - lam_init))
        else:
            out = jnp.concatenate([a1, a2], axis=0)
        out_ref[0, pl.ds(pl.multiple_of(qi * t, t), t), :] = out.T.astype(BF16)

    def consume(pair, buf, j, diagonal):
        qi, kb = pair
        first = kb == 0
        s = s_bufs[buf][j]
        if diagonal:
            kr = lax.broadcasted_iota(jnp.int32, (t, t), 0)
            qc = lax.broadcasted_iota(jnp.int32, (t, t), 1)
            s = jnp.where(kr <= qc, s, NEG)
            block_max = jnp.max(s, axis=0, keepdims=True)
        else:
            block_max = mx_bufs[buf][j]
        m = jnp.where(first, NEG, m_ref[j])
        l = jnp.where(first, 0.0, l_ref[j])
        m_new = jnp.maximum(m, block_max)
        alpha = jnp.exp2(m - m_new)
        p = jnp.exp2((s - m_new).astype(BF16))
        m_ref[j] = m_new
        if mode == "diff":
            vt = vt_ref[0, kb]
        else:
            vt = vt_ref[0, kb, j * v_rows:(j + 1) * v_rows, :]
        pv = _dot(vt, p)
        l_ref[j] = alpha * l + pv[v_dims:v_dims + 1]
        acc_ref[j] = alpha * acc_ref[j] + pv[:v_dims]
        if diagonal and j == 1:
            finish_tile(qi)

    def advance(pair):
        qi, kb = pair
        end = kb == qi
        nqi, nkb = jnp.where(end, qi + 1, qi), jnp.where(end, 0, kb + 1)
        done = nqi >= n_tiles
        return jnp.where(done, qi, nqi), jnp.where(done, kb, nkb)

    acc_ref[...] = jnp.zeros(acc_ref.shape, F32)
    pair0 = (jnp.int32(0), jnp.int32(0))
    pair1 = advance(pair0)
    for j in range(2):
        score(pair0, 0, j)
        score(pair1, 1, j)

    def half_trip(pa, pb, half):
        pc = advance(pb)
        pd = advance(pc)
        diag_a, diag_b = pa[0] == pa[1], pb[0] == pb[1]
        ra, rb, wc, wd = 2 * half, 2 * half + 1, 2 - 2 * half, 3 - 2 * half

        def run(da, db):
            score(pc, wc, 0)
            score(pc, wc, 1)
            consume(pa, ra, 0, da)
            score(pd, wd, 0)
            consume(pa, ra, 1, da)
            consume(pb, rb, 0, db)
            score(pd, wd, 1)
            consume(pb, rb, 1, db)

        @pl.when(diag_a | diag_b)
        def _():
            pl.when(diag_a)(lambda: run(True, False))
            pl.when(diag_b)(lambda: run(False, True))

        pl.when(jnp.logical_not(diag_a | diag_b))(lambda: run(False, False))
        return pc, pd

    def body(_, carry):
        pa, pb = carry[:2], carry[2:]
        for half in range(2):
            pa, pb = half_trip(pa, pb, half)
        return (*pa, *pb)

    assert n_pairs % 4 == 0
    lax.fori_loop(0, n_pairs // 4, body, (*pair0, *pair1))


def _causal_attention(qt, k, vt, mode, extras=(), lam_init=0.0):
    b, s, _ = k.shape
    t = ATTN_TILE
    assert qt.shape[3] == t and vt.shape[3] == t
    qw = 128 if mode == "diff" else 2 * D_PAD_DIM
    groups = k.shape[2] // qw
    in_specs = [
        pl.BlockSpec((1, s // t, qw, t), lambda bi, h: (bi, 0, h, 0)),
        pl.BlockSpec((1, s, qw), lambda bi, h: (bi, 0, h)),
        pl.BlockSpec((1, s // t, vt.shape[2] // groups, t), lambda bi, h: (bi, 0, h, 0)),
    ] + [_const_spec(e.shape) for e in extras]
    stats = [pltpu.VMEM((2, 1, t), F32)] * 2
    return pl.pallas_call(
        functools.partial(_causal_kernel, mode=mode, lam_init=lam_init, n_tiles=s // t),
        grid=(b, groups),
        in_specs=in_specs,
        out_specs=pl.BlockSpec((1, s, 128), lambda bi, h: (bi, 0, h)),
        out_shape=jax.ShapeDtypeStruct((b, s, groups * 128), BF16),
        scratch_shapes=[
            *[pltpu.VMEM((2, t, t), F32)] * 4,
            *stats, *stats,
            *stats,
            pltpu.VMEM((2, 128 if mode == "diff" else D_V_DIM, t), F32),
        ],
        compiler_params=_cparams(("parallel", "parallel")),
        name=f"causal_attn_{mode}",
    )(qt, k, vt, *extras)


def _cd_out_kernel(x_ref, yc_ref, yd_ref, wc_ref, wd_ref, out_ref):
    out_ref[...] = x_ref[...] + _dot(yc_ref[...], wc_ref[...]) + _dot(yd_ref[...], wd_ref[...])


def _cd_out(x2, yc, yd, w_out):
    t = x2.shape[0]
    row = lambda w: pl.BlockSpec((ROW_TILE, w), lambda i: (i, 0))
    dvw = D_HEADS * D_V_DIM
    return pl.pallas_call(
        _cd_out_kernel,
        grid=(t // ROW_TILE,),
        in_specs=[row(D_MODEL), row(C_WIDTH), row(dvw),
                  _const_spec((C_WIDTH, D_MODEL)), _const_spec((dvw, D_MODEL))],
        out_specs=row(D_MODEL),
        out_shape=jax.ShapeDtypeStruct((t, D_MODEL), F32),
        compiler_params=_cparams(("parallel",)),
        name="cd_out_proj",
    )(x2, yc, yd, w_out[:C_WIDTH].astype(BF16), w_out[C_WIDTH:].astype(BF16))


def kernel(x, mem, positions, norm_mix_g, norm_cross_g, norm_mem_g, cross_wq, cross_wkv, cross_wo,
           norm_mlp_g, mlp_w1, mlp_w2, ab_w_in, ab_w_out, ab_conv_w, ab_conv_b, ab_ln_g, ab_ln_b,
           cd_w_in, cd_w_out, diff_lq1, diff_lk1, diff_lq2, diff_lk2, diff_subln_g, mla_q_norm_g,
           mla_kv_norm_g, mla_w_uq, mla_w_uk, mla_w_uv, final_norm_g):
    b, s, d = x.shape
    t = b * s
    mem_len = mem.shape[1]
    depth = norm_mix_g.shape[0]
    xw = X_HEADS * X_HEAD_DIM
    x2 = x.reshape(t, d)
    for i in range(depth):
        j = i // 2
        if i % 2 == 0:
            *views, glu = _ab_in(x2.reshape(b, s, d), norm_mix_g[i], ab_w_in[j])
            outs, lses = zip(*[_dilated_pattern(view, dil)
                               for view, (_, dil) in zip(views, A_PATTERNS)])
            yb = _conv_module(glu, ab_conv_w[j], ab_conv_b[j], ab_ln_g[j], ab_ln_b[j])
            x2 = _ab_out(x2.reshape(b, s, d), outs, lses, yb, ab_w_out[j]).reshape(t, d)
        else:
            qc, kc, vc, qd, kd, vd = _cd_in(
                x2.reshape(b, s, d), positions, norm_mix_g[i], cd_w_in[j], mla_q_norm_g[j],
                mla_kv_norm_g[j], mla_w_uq[j], mla_w_uk[j], mla_w_uv[j])
            lam_init = 0.8 - 0.6 * math.exp(-0.3 * i)
            vec = lambda a: a.reshape(1, -1)
            yc = _causal_attention(
                qc, kc, vc, "diff",
                extras=(vec(diff_lq1[j]), vec(diff_lk1[j]), vec(diff_lq2[j]), vec(diff_lk2[j]),
                        diff_subln_g[j].reshape(-1, 1)),
                lam_init=lam_init)
            yd = _causal_attention(qd, kd, vd, "mla")
            x2 = _cd_out(x2, yc.reshape(t, C_WIDTH), yd.reshape(t, D_HEADS * D_V_DIM), cd_w_out[j])
        km, vm = _mem_kv(mem.reshape(b * mem_len, d), norm_mem_g[i], cross_wkv[i])
        x2 = _cross_attention(x2.reshape(b, s, d), norm_cross_g[i], cross_wq[i],
                              km.reshape(b, mem_len, xw), vm.reshape(b, mem_len, xw),
                              cross_wo[i]).reshape(t, d)
        x2 = _mlp(x2, norm_mlp_g[i], mlp_w1[i], mlp_w2[i], final_norm_g, i == depth - 1)
    return x2.reshape(b, s, d)
```

```python
import functools
import math

import jax
import jax.numpy as jnp
from jax import lax
from jax.experimental import pallas as pl
from jax.experimental.pallas import tpu as pltpu

F32 = jnp.float32
BF16 = jnp.bfloat16

D_MODEL = 1024
EPS = 1e-6
LOG2E = math.log2(math.e)
NEG = -1e30

A_HEADS = 8
A_HEAD_DIM = 64
A_WIDTH = A_HEADS * A_HEAD_DIM
A_PATTERNS = ((128, 1), (512, 4), (2048, 16))
A_BLK = 128
assert all(window // dilation == A_BLK for window, dilation in A_PATTERNS)
B_CHANNELS = 512
B_CONV_WIDTH = 31
B_HALO = 32
C_HEADS = 4
C_HEAD_DIM = 64
C_WIDTH = C_HEADS * 2 * C_HEAD_DIM
D_HEADS = 8
D_NOPE_DIM = 64
D_ROPE_DIM = 32
D_V_DIM = 64
D_Q_RANK = 384
D_KV_RANK = 256
D_PAD_DIM = 128
ROPE_THETA = 10000.0
X_HEADS = 4
X_HEAD_DIM = 128
D_FF = 4 * D_MODEL
FF_CHUNK = 1024

ROW_TILE = 512
ATTN_TILE = ROW_TILE
VMEM_LIMIT = 56 * 1024 * 1024


def _cparams(sem):
    return pltpu.CompilerParams(dimension_semantics=sem, vmem_limit_bytes=VMEM_LIMIT)


def _rms(x, g):
    return x * lax.rsqrt(jnp.mean(x * x, axis=-1, keepdims=True) + EPS) * g


def _dot(a, b):
    return jnp.dot(a, b, preferred_element_type=F32)


def _dot_t(a, b):
    return lax.dot_general(a, b, (((1,), (1,)), ((), ())), preferred_element_type=F32)


def _const_spec(shape):
    nd = len(shape)
    return pl.BlockSpec(shape, lambda *_: (0,) * nd)


def _regather_dilations():
    dils = sorted(d for _, d in A_PATTERNS if d > 1)
    return [e for e in dils if any(d > e and d % e == 0 for d in dils)]


def _ab_in_kernel(x_ref, g_ref, wqkv_ref, wu_ref, wg_ref, *rest):
    n_keep = len(_regather_dilations())
    views, glu_ref = rest[:-2 - n_keep], rest[-2 - n_keep]
    qkv_scr, kept = rest[-1 - n_keep], dict(zip(_regather_dilations(), rest[len(rest) - n_keep:]))
    n = _rms(x_ref[0], g_ref[...]).astype(BF16)
    width = 3 * A_WIDTH
    dense = [v for (_, d), v in zip(A_PATTERNS, views) if d == 1]
    chunk = 256
    assert A_WIDTH % chunk == 0
    for c0 in range(0, width, chunk):
        part = _dot(n, wqkv_ref[:, c0:c0 + chunk])
        if c0 < A_WIDTH:
            part = part * (A_HEAD_DIM ** -0.5 * LOG2E)
        for c in range(c0 // 128, (c0 + chunk) // 128):
            qkv_scr[c] = part[:, c * 128 - c0:(c + 1) * 128 - c0]
        for view_ref in dense:
            view_ref[0, :, c0:c0 + chunk] = part.astype(BF16)
    sources = {1: qkv_scr}
    for d, view_ref in sorted((d, v) for (_, d), v in zip(A_PATTERNS, views) if d > 1):
        e = max(e for e in sources if d % e == 0)
        step, n_d, n_e = d // e, ROW_TILE // d, ROW_TILE // e
        for r in range(d):
            start = (r % e) * n_e + r // e
            for c in range(width // 128):
                rows = sources[e][c, pl.ds(start, n_d, stride=step), :]
                view_ref[0, :, r * width + c * 128:r * width + (c + 1) * 128] = rows.astype(BF16)
                if d in kept:
                    kept[d][c, r * n_d:(r + 1) * n_d, :] = rows
        if d in kept:
            sources[d] = kept[d]
    u = _dot(n, wu_ref[...])
    gate = _dot(n, wg_ref[...])
    glu_ref[0] = u * jax.nn.sigmoid(gate)


def _ab_in(x3, g, w_in):
    b, s, _ = x3.shape
    width = 3 * A_WIDTH
    wqkv = w_in[:, :width].astype(BF16)
    wu = w_in[:, width:width + B_CHANNELS].astype(BF16)
    wg = w_in[:, width + B_CHANNELS:].astype(BF16)
    row = lambda rows, w: pl.BlockSpec((1, rows, w), lambda bi, i: (bi, i, 0))
    dils = [d for _, d in A_PATTERNS]
    return pl.pallas_call(
        _ab_in_kernel,
        grid=(b, s // ROW_TILE),
        in_specs=[
            row(ROW_TILE, D_MODEL),
            _const_spec((1, D_MODEL)),
            _const_spec((D_MODEL, width)),
            _const_spec((D_MODEL, B_CHANNELS)),
            _const_spec((D_MODEL, B_CHANNELS)),
        ],
        out_specs=[row(ROW_TILE // d, d * width) for d in dils] + [row(ROW_TILE, B_CHANNELS)],
        out_shape=[jax.ShapeDtypeStruct((b, s // d, d * width), BF16) for d in dils]
        + [jax.ShapeDtypeStruct((b, s, B_CHANNELS), F32)],
        scratch_shapes=[pltpu.VMEM((width // 128, ROW_TILE, 128), F32)]
        * (1 + len(_regather_dilations())),
        compiler_params=_cparams(("parallel", "parallel")),
        name="ab_in_proj",
    )(x3, g.reshape(1, D_MODEL), wqkv, wu, wg)


def _dilated_kernel(q_ref, kp_ref, kc_ref, vp_ref, vc_ref, o_ref, lse_ref, s_ref, *, rows):
    first_tile = pl.program_id(2) == 0
    qi = lax.broadcasted_iota(jnp.int32, (A_BLK, 2 * A_BLK), 0)
    kj = lax.broadcasted_iota(jnp.int32, (A_BLK, 2 * A_BLK), 1)
    band = ((kj < A_BLK) & (kj >= qi)) | ((kj >= A_BLK) & (kj - A_BLK <= qi))
    lane = lax.broadcasted_iota(jnp.int32, (A_BLK, 128), 1)
    low = lane < A_HEAD_DIM
    pairs = A_WIDTH // 128
    nblk = rows // A_BLK

    def blk(i):
        return slice(i * A_BLK, (i + 1) * A_BLK)

    def prev_and_cur(i, prev_ref, cur_ref, cols):
        prev = prev_ref[0, :, cols] if i == 0 else cur_ref[0, blk(i - 1), cols]
        return jnp.concatenate([prev, cur_ref[0, blk(i), cols]], axis=0)

    def scores(i):
        for pr in range(pairs):
            cols = slice(pr * 128, (pr + 1) * 128)
            q = q_ref[0, blk(i), cols]
            k = prev_and_cur(i, kp_ref, kc_ref, cols)
            zero = jnp.zeros_like(q)
            both = jnp.concatenate([jnp.where(low, q, zero), jnp.where(low, zero, q)], axis=0)
            s = _dot_t(both, k)
            s_ref[i % 2, 2 * pr] = s[:A_BLK]
            s_ref[i % 2, 2 * pr + 1] = s[A_BLK:]

    def consume(i):
        valid = band & ((kj >= A_BLK) | jnp.logical_not(first_tile)) if i == 0 else band
        for pr in range(pairs):
            cols = slice(pr * 128, (pr + 1) * 128)
            v = prev_and_cur(i, vp_ref, vc_ref, cols)
            ps, ls, lses = [], [], []
            for half in range(2):
                s = jnp.where(valid, s_ref[i % 2, 2 * pr + half], NEG)
                m = jnp.max(s, axis=-1, keepdims=True)
                p = jnp.exp2(s - m)
                l = jnp.sum(p, axis=-1, keepdims=True)
                ps.append(p.astype(BF16))
                ls.append(l)
                lses.append(m + jnp.log2(l))
            pv = _dot(jnp.concatenate(ps, axis=0), v)
            out = jnp.where(low, pv[:A_BLK] / ls[0], pv[A_BLK:] / ls[1])
            o_ref[0, blk(i), cols] = out.astype(BF16)
            lse_ref[0, blk(i), cols] = jnp.where(low, lses[0], lses[1])

    scores(0)
    for i in range(nblk):
        if i + 1 < nblk:
            scores(i + 1)
        consume(i)


def _dilated_pattern(view, dilation):
    b, sub, _ = view.shape
    rows = min(sub, 512)
    assert sub % rows == 0 and rows % A_BLK == 0
    blocks_per_tile = rows // A_BLK
    prev_row = lambda n: jnp.maximum(n * blocks_per_tile - 1, 0)
    cur_spec = lambda part: pl.BlockSpec((1, rows, A_WIDTH),
                                         lambda bi, r, n: (bi, n, 3 * r + part))
    prev_spec = lambda part: pl.BlockSpec((1, A_BLK, A_WIDTH),
                                          lambda bi, r, n: (bi, prev_row(n), 3 * r + part))
    out_spec = pl.BlockSpec((1, rows, A_WIDTH), lambda bi, r, n: (bi, n, r))
    out_dims = (b, sub, dilation * A_WIDTH)
    return pl.pallas_call(
        functools.partial(_dilated_kernel, rows=rows),
        grid=(b, dilation, sub // rows),
        in_specs=[cur_spec(0), prev_spec(1), cur_spec(1), prev_spec(2), cur_spec(2)],
        out_specs=[out_spec, out_spec],
        out_shape=[jax.ShapeDtypeStruct(out_dims, BF16), jax.ShapeDtypeStruct(out_dims, F32)],
        scratch_shapes=[pltpu.VMEM((2, A_HEADS, A_BLK, 2 * A_BLK), F32)],
        compiler_params=_cparams(("parallel", "parallel", "arbitrary")),
        name=f"dilated_attn_d{dilation}",
    )(view, view, view, view, view)


CONV_ROWS = 128
SUBLANES = 8


def _conv_kernel(halo_ref, cur_ref, w_ref, cb_ref, g_ref, b_ref, out_ref, buf_ref, *, rows):
    first_tile = pl.program_id(1) == 0
    halo = halo_ref[0]
    buf_ref[0:B_HALO, :] = jnp.where(first_tile, jnp.zeros_like(halo), halo)
    buf_ref[B_HALO:B_HALO + rows, :] = cur_ref[0]
    lead = B_HALO - (B_CONV_WIDTH - 1)

    def chunk(c, carry):
        base = pl.multiple_of(c * CONV_ROWS, CONV_ROWS)
        span = CONV_ROWS + B_HALO
        accs = []
        for c in range(B_CHANNELS // 128):
            cols = slice(c * 128, (c + 1) * 128)
            win = buf_ref[pl.ds(base, span), cols]
            acc = jnp.zeros((CONV_ROWS, 128), F32)
            for r in range(SUBLANES):
                shifted = pltpu.roll(win, span - r, 0) if r else win
                for tap in range(B_CONV_WIDTH):
                    if (lead + tap) % SUBLANES == r:
                        a = (lead + tap) // SUBLANES * SUBLANES
                        acc = acc + shifted[a:a + CONV_ROWS, :] * w_ref[tap:tap + 1, cols]
            accs.append(acc)
        y = jnp.concatenate(accs, axis=-1) + cb_ref[...]
        mu = jnp.mean(y, axis=-1, keepdims=True)
        yc = y - mu
        var = jnp.mean(yc * yc, axis=-1, keepdims=True)
        z = yc * lax.rsqrt(var + EPS) * g_ref[...] + b_ref[...]
        out_ref[0, pl.ds(base, CONV_ROWS), :] = (z * jax.nn.sigmoid(z)).astype(BF16)
        return carry

    lax.fori_loop(0, rows // CONV_ROWS, chunk, 0)


def _conv_module(glu, conv_w, conv_b, ln_g, ln_b):
    b, s, c = glu.shape
    rows = ROW_TILE
    halo_blocks = rows // B_HALO
    vec = lambda a: a.reshape(1, c)
    return pl.pallas_call(
        functools.partial(_conv_kernel, rows=rows),
        grid=(b, s // rows),
        in_specs=[
            pl.BlockSpec((1, B_HALO, c), lambda bi, n: (bi, jnp.maximum(n * halo_blocks - 1, 0), 0)),
            pl.BlockSpec((1, rows, c), lambda bi, n: (bi, n, 0)),
            _const_spec((B_CONV_WIDTH, c)),
            _const_spec((1, c)), _const_spec((1, c)), _const_spec((1, c)),
        ],
        out_specs=pl.BlockSpec((1, rows, c), lambda bi, n: (bi, n, 0)),
        out_shape=jax.ShapeDtypeStruct((b, s, c), BF16),
        scratch_shapes=[pltpu.VMEM((B_HALO + rows, c), F32)],
        compiler_params=_cparams(("parallel", "arbitrary")),
        name="conformer_conv",
    )(glu, glu, conv_w.reshape(B_CONV_WIDTH, c), vec(conv_b), vec(ln_g), vec(ln_b))


def _finer_dilation(d):
    return max([e for _, e in A_PATTERNS if 1 < e < d and d % e == 0], default=1)


def _ab_out_kernel(x_ref, *rest):
    n_pat = len(A_PATTERNS)
    o_refs, l_refs = rest[:n_pat], rest[n_pat:2 * n_pat]
    yb_ref, wa_ref, wb_ref, out_ref = rest[2 * n_pat:2 * n_pat + 4]
    scr = list(rest[2 * n_pat + 4:])
    tiles = A_WIDTH // 128
    gather = lambda ref: jnp.concatenate([ref[c] for c in range(tiles)], axis=-1)
    outs, lses = [], []
    for idx, (_, d) in enumerate(A_PATTERNS):
        if d == 1:
            outs.append(o_refs[idx][0].astype(F32))
            lses.append(l_refs[idx][0])
            continue
        o_scr, l_scr = scr.pop(0), scr.pop(0)
        e = _finer_dilation(d)
        o_mid, l_mid = (scr.pop(0), scr.pop(0)) if e > 1 else (o_scr, l_scr)
        step, n_d, n_e = d // e, ROW_TILE // d, ROW_TILE // e
        for r in range(d):
            dst = pl.ds((r % e) * n_e + r // e, n_d, stride=step)
            for c in range(tiles):
                cols = slice(r * A_WIDTH + c * 128, r * A_WIDTH + (c + 1) * 128)
                o_mid[c, dst, :] = o_refs[idx][0, :, cols].astype(F32)
                l_mid[c, dst, :] = l_refs[idx][0, :, cols]
        if e > 1:
            for q in range(e):
                for c in range(tiles):
                    o_scr[c, pl.ds(q, n_e, stride=e), :] = o_mid[c, q * n_e:(q + 1) * n_e, :]
                    l_scr[c, pl.ds(q, n_e, stride=e), :] = l_mid[c, q * n_e:(q + 1) * n_e, :]
        outs.append(gather(o_scr))
        lses.append(gather(l_scr))
    m = functools.reduce(jnp.maximum, lses)
    es = [jnp.exp2(l - m) for l in lses]
    ya = sum(e * o for e, o in zip(es, outs)) / sum(es)
    out_ref[0] = (x_ref[0] + _dot(ya.astype(BF16), wa_ref[...])
                  + _dot(yb_ref[0], wb_ref[...]))


def _ab_out(x3, outs, lses, yb, w_out):
    b, s, _ = x3.shape
    row = lambda rows, w: pl.BlockSpec((1, rows, w), lambda bi, i: (bi, i, 0))
    views = [row(ROW_TILE // d, d * A_WIDTH) for _, d in A_PATTERNS]
    scratch = []
    for _, d in A_PATTERNS:
        if d > 1:
            scratch += [pltpu.VMEM((A_WIDTH // 128, ROW_TILE, 128), F32)] * (
                4 if _finer_dilation(d) > 1 else 2)
    return pl.pallas_call(
        _ab_out_kernel,
        grid=(b, s // ROW_TILE),
        in_specs=[row(ROW_TILE, D_MODEL)] + views + views + [
            row(ROW_TILE, B_CHANNELS),
            _const_spec((A_WIDTH, D_MODEL)), _const_spec((B_CHANNELS, D_MODEL))],
        out_specs=row(ROW_TILE, D_MODEL),
        out_shape=jax.ShapeDtypeStruct((b, s, D_MODEL), F32),
        scratch_shapes=scratch,
        compiler_params=_cparams(("parallel", "parallel")),
        name="ab_out_proj",
    )(x3, *outs, *lses, yb, w_out[:A_WIDTH].astype(BF16), w_out[A_WIDTH:].astype(BF16))


def _mem_kv_kernel(mem_ref, g_ref, wk_ref, wv_ref, k_ref, v_ref):
    n = _rms(mem_ref[...], g_ref[...]).astype(BF16)
    k_ref[...] = _dot(n, wk_ref[...]).astype(BF16)
    v_ref[...] = _dot(n, wv_ref[...]).astype(BF16)


def _mem_kv(mem2, g, wkv):
    rows = mem2.shape[0]
    w = X_HEADS * X_HEAD_DIM
    return pl.pallas_call(
        _mem_kv_kernel,
        grid=(1,),
        in_specs=[_const_spec((rows, D_MODEL)), _const_spec((1, D_MODEL)),
                  _const_spec((D_MODEL, w)), _const_spec((D_MODEL, w))],
        out_specs=[_const_spec((rows, w)), _const_spec((rows, w))],
        out_shape=[jax.ShapeDtypeStruct((rows, w), BF16)] * 2,
        compiler_params=_cparams(("arbitrary",)),
        name="mem_kv_proj",
    )(mem2, g.reshape(1, D_MODEL), wkv[:, :w].astype(BF16), wkv[:, w:].astype(BF16))


def _cross_kernel(x_ref, g_ref, wq_ref, k_ref, v_ref, wo_ref, out_ref):
    x = x_ref[0]
    n = _rms(x, g_ref[...]).astype(BF16)
    q = (_dot(n, wq_ref[...]) * (X_HEAD_DIM ** -0.5)).astype(BF16)
    heads = []
    for h in range(X_HEADS):
        cols = slice(h * X_HEAD_DIM, (h + 1) * X_HEAD_DIM)
        s = _dot_t(q[:, cols], k_ref[0, :, cols])
        m = jnp.max(s, axis=-1, keepdims=True)
        p = jnp.exp(s - m)
        l = jnp.sum(p, axis=-1, keepdims=True)
        heads.append((_dot(p.astype(BF16), v_ref[0, :, cols]) / l).astype(BF16))
    o = jnp.concatenate(heads, axis=-1)
    out_ref[0] = x + _dot(o, wo_ref[...])


def _cross_attention(x3, g, wq, k, v, wo):
    b, s, _ = x3.shape
    m = k.shape[1]
    w = X_HEADS * X_HEAD_DIM
    return pl.pallas_call(
        _cross_kernel,
        grid=(b, s // ROW_TILE),
        in_specs=[
            pl.BlockSpec((1, ROW_TILE, D_MODEL), lambda bi, i: (bi, i, 0)),
            _const_spec((1, D_MODEL)),
            _const_spec((D_MODEL, w)),
            pl.BlockSpec((1, m, w), lambda bi, i: (bi, 0, 0)),
            pl.BlockSpec((1, m, w), lambda bi, i: (bi, 0, 0)),
            _const_spec((w, D_MODEL)),
        ],
        out_specs=pl.BlockSpec((1, ROW_TILE, D_MODEL), lambda bi, i: (bi, i, 0)),
        out_shape=jax.ShapeDtypeStruct((b, s, D_MODEL), F32),
        compiler_params=_cparams(("parallel", "parallel")),
        name="cross_attn",
    )(x3, g.reshape(1, D_MODEL), wq.astype(BF16), k, v, wo.astype(BF16))


def _mlp_kernel(x_ref, g_ref, w1_ref, w2_ref, gf_ref, out_ref, *, final_norm):
    x = x_ref[...]
    n = _rms(x, g_ref[...]).astype(BF16)
    acc = x
    for c in range(D_FF // FF_CHUNK):
        cols = slice(c * FF_CHUNK, (c + 1) * FF_CHUNK)
        h = jnp.maximum(_dot(n, w1_ref[:, cols]), 0.0)
        acc = acc + _dot((h * h).astype(BF16), w2_ref[cols, :])
    if final_norm:
        acc = _rms(acc, gf_ref[...])
    out_ref[...] = acc


def _mlp(x2, g, w1, w2, gf, final_norm):
    t = x2.shape[0]
    return pl.pallas_call(
        functools.partial(_mlp_kernel, final_norm=final_norm),
        grid=(t // ROW_TILE,),
        in_specs=[
            pl.BlockSpec((ROW_TILE, D_MODEL), lambda i: (i, 0)),
            _const_spec((1, D_MODEL)),
            _const_spec((D_MODEL, D_FF)),
            _const_spec((D_FF, D_MODEL)),
            _const_spec((1, D_MODEL)),
        ],
        out_specs=pl.BlockSpec((ROW_TILE, D_MODEL), lambda i: (i, 0)),
        out_shape=jax.ShapeDtypeStruct((t, D_MODEL), F32),
        compiler_params=_cparams(("parallel",)),
        name="mlp_final" if final_norm else "mlp",
    )(x2, g.reshape(1, D_MODEL), w1.astype(BF16), w2.astype(BF16), gf.reshape(1, D_MODEL))


def _rope_rows(xt, cos, sin):
    lo, half = D_NOPE_DIM, D_ROPE_DIM // 2
    x1, x2 = xt[lo:lo + half], xt[lo + half:lo + 2 * half]
    return jnp.concatenate(
        [xt[:lo], x1 * cos - x2 * sin, x2 * cos + x1 * sin, xt[lo + 2 * half:]], axis=0)


ONES_ROWS = 16


def _with_ones_rows(vt, head_rows):
    ones = jnp.ones((ONES_ROWS, vt.shape[1]), F32)
    parts = []
    for r0 in range(0, vt.shape[0], head_rows):
        parts += [vt[r0:r0 + head_rows], ones]
    return jnp.concatenate(parts, axis=0).astype(BF16)


def _cd_in_kernel(x_ref, pos_ref, g_ref, freq_ref, wqkv_ref, wcq_ref, wckv_ref, wkr_ref,
                  qng_ref, kvng_ref, wuq_ref, wuk_ref, wuv_ref,
                  qct_ref, kc_ref, vct_ref, qdt_ref, kd_ref, vdt_ref):
    n = _rms(x_ref[0], g_ref[...]).astype(BF16)
    qc = _dot(n, wqkv_ref[:, :C_WIDTH])
    qct_ref[0, 0] = (qc * (C_HEAD_DIM ** -0.5 * LOG2E)).T.astype(BF16)
    vct_ref[0, 0] = _with_ones_rows(_dot(n, wqkv_ref[:, 2 * C_WIDTH:]).T, 2 * C_HEAD_DIM)

    ang = freq_ref[...] * pos_ref[0, 0].astype(F32)
    cos, sin = jnp.cos(ang), jnp.sin(ang)

    cq = _rms(_dot(n, wcq_ref[...]), qng_ref[...]).astype(BF16)
    q = _dot(cq, wuq_ref[...])
    ckv = _rms(_dot(n, wckv_ref[...]), kvng_ref[...]).astype(BF16)
    k_nope = _dot(ckv, wuk_ref[...])
    vdt_ref[0, 0] = _with_ones_rows(_dot(ckv, wuv_ref[...]).T, D_V_DIM)
    k_rope = _rope_rows(_dot(n, wkr_ref[...]).T, cos, sin).T
    scale = (D_NOPE_DIM + D_ROPE_DIM) ** -0.5 * LOG2E
    for h in range(D_HEADS):
        cols = slice(h * D_PAD_DIM, (h + 1) * D_PAD_DIM)
        qdt_ref[0, 0, cols, :] = (_rope_rows(q[:, cols].T, cos, sin) * scale).astype(BF16)
        kd_ref[0, :, cols] = (k_nope[:, cols] + k_rope).astype(BF16)
    kc_ref[0] = _dot(n, wqkv_ref[:, C_WIDTH:2 * C_WIDTH]).astype(BF16)


def _pad_heads(w, heads, width):
    k = w.shape[0]
    w = w.reshape(k, heads, width)
    return jnp.pad(w, ((0, 0), (0, 0), (0, D_PAD_DIM - width))).reshape(k, heads * D_PAD_DIM)


def _cd_in(x3, positions, g, w_in, q_norm_g, kv_norm_g, w_uq, w_uk, w_uv):
    b, s, _ = x3.shape
    o3 = 3 * C_WIDTH
    o4 = o3 + D_Q_RANK
    o5 = o4 + D_KV_RANK
    wqkv = w_in[:, :o3].astype(BF16)
    wcq = w_in[:, o3:o4].astype(BF16)
    wckv = w_in[:, o4:o5].astype(BF16)
    wkr = jnp.pad(w_in[:, o5:], ((0, 0), (D_NOPE_DIM, D_PAD_DIM - D_NOPE_DIM - D_ROPE_DIM))).astype(BF16)
    wuq = _pad_heads(w_uq, D_HEADS, D_NOPE_DIM + D_ROPE_DIM).astype(BF16)
    wuk = _pad_heads(w_uk, D_HEADS, D_NOPE_DIM).astype(BF16)
    half = D_ROPE_DIM // 2
    freq = (ROPE_THETA ** (-jnp.arange(half, dtype=F32) / half)).reshape(half, 1)
    dw = D_HEADS * D_PAD_DIM
    vc_rows = C_HEADS * (2 * C_HEAD_DIM + ONES_ROWS)
    vd_rows = D_HEADS * (D_V_DIM + ONES_ROWS)
    nt = s // ROW_TILE
    row = lambda w: pl.BlockSpec((1, ROW_TILE, w), lambda bi, i: (bi, i, 0))
    colmajor = lambda w: pl.BlockSpec((1, 1, w, ROW_TILE), lambda bi, i: (bi, i, 0, 0))
    return pl.pallas_call(
        _cd_in_kernel,
        grid=(b, s // ROW_TILE),
        in_specs=[
            row(D_MODEL), colmajor(1), _const_spec((1, D_MODEL)), _const_spec((half, 1)),
            _const_spec((D_MODEL, o3)), _const_spec((D_MODEL, D_Q_RANK)),
            _const_spec((D_MODEL, D_KV_RANK)), _const_spec((D_MODEL, D_PAD_DIM)),
            _const_spec((1, D_Q_RANK)), _const_spec((1, D_KV_RANK)),
            _const_spec((D_Q_RANK, dw)), _const_spec((D_KV_RANK, dw)),
            _const_spec((D_KV_RANK, D_HEADS * D_V_DIM)),
        ],
        out_specs=[colmajor(C_WIDTH), row(C_WIDTH), colmajor(vc_rows), colmajor(dw), row(dw),
                   colmajor(vd_rows)],
        out_shape=[
            jax.ShapeDtypeStruct((b, nt, C_WIDTH, ROW_TILE), BF16),
            jax.ShapeDtypeStruct((b, s, C_WIDTH), BF16),
            jax.ShapeDtypeStruct((b, nt, vc_rows, ROW_TILE), BF16),
            jax.ShapeDtypeStruct((b, nt, dw, ROW_TILE), BF16),
            jax.ShapeDtypeStruct((b, s, dw), BF16),
            jax.ShapeDtypeStruct((b, nt, vd_rows, ROW_TILE), BF16),
        ],
        compiler_params=_cparams(("parallel", "parallel")),
        name="cd_in_proj",
    )(x3, positions.reshape(b, nt, 1, ROW_TILE), g.reshape(1, D_MODEL), freq, wqkv, wcq, wckv, wkr,
      q_norm_g.reshape(1, D_Q_RANK), kv_norm_g.reshape(1, D_KV_RANK), wuq, wuk,
      w_uv.astype(BF16))


def _causal_kernel(qt_ref, k_ref, vt_ref, *rest, mode, lam_init, n_tiles):
    if mode == "diff":
        lq1_ref, lk1_ref, lq2_ref, lk2_ref, g_ref, out_ref = rest[:6]
    else:
        out_ref = rest[0]
    s_bufs, mx_bufs = rest[-11:-7], rest[-7:-3]
    m_ref, l_ref, acc_ref = rest[-3:]
    t = ATTN_TILE
    v_dims = 2 * C_HEAD_DIM if mode == "diff" else D_V_DIM
    v_rows = v_dims + ONES_ROWS
    n_pairs = n_tiles * (n_tiles + 1) // 2
    assert n_tiles >= 2 and n_pairs % 2 == 0
    feat = lax.broadcasted_iota(jnp.int32, (128, t), 0)

    def score(pair, buf, j):
        qi, kb = pair
        rows = pl.ds(pl.multiple_of(kb * t, t), t)
        if mode == "diff":
            qt = qt_ref[0, qi]
            mine = (feat < C_HEAD_DIM) if j == 0 else (feat >= C_HEAD_DIM)
            q, k = jnp.where(mine, qt, jnp.zeros_like(qt)), k_ref[0, rows, :]
        else:
            cols = slice(j * D_PAD_DIM, (j + 1) * D_PAD_DIM)
            q, k = qt_ref[0, qi, cols, :], k_ref[0, rows, cols]
        s = _dot(k, q)
        s_bufs[buf][j] = s
        mx_bufs[buf][j] = jnp.max(s, axis=0, keepdims=True)

    def finish_tile(qi):
        a1 = acc_ref[0] / l_ref[0]
        a2 = acc_ref[1] / l_ref[1]
        if mode == "diff":
            lam = (jnp.exp(jnp.sum(lq1_ref[...] * lk1_ref[...], axis=-1, keepdims=True))
                   - jnp.exp(jnp.sum(lq2_ref[...] * lk2_ref[...], axis=-1, keepdims=True))
                   + lam_init)
            d = a1 - lam * a2
            out = (d * lax.rsqrt(jnp.mean(d * d, axis=0, keepdims=True) + EPS) * g_ref[...]
                   * (1.0 - lam_init))
        else:
            out = jnp.concatenate([a1, a2], axis=0)
        out_ref[0, pl.ds(pl.multiple_of(qi * t, t), t), :] = out.T.astype(BF16)

    def consume(pair, buf, j, diagonal):
        qi, kb = pair
        first = kb == 0
        s = s_bufs[buf][j]
        if diagonal:
            kr = lax.broadcasted_iota(jnp.int32, (t, t), 0)
            qc = lax.broadcasted_iota(jnp.int32, (t, t), 1)
            s = jnp.where(kr <= qc, s, NEG)
            block_max = jnp.max(s, axis=0, keepdims=True)
        else:
            block_max = mx_bufs[buf][j]
        m = jnp.where(first, NEG, m_ref[j])
        l = jnp.where(first, 0.0, l_ref[j])
        m_new = jnp.maximum(m, block_max)
        alpha = jnp.exp2(m - m_new)
        p = jnp.exp2((s - m_new).astype(BF16))
        m_ref[j] = m_new
        if mode == "diff":
            vt = vt_ref[0, kb]
        else:
            vt = vt_ref[0, kb, j * v_rows:(j + 1) * v_rows, :]
        pv = _dot(vt, p)
        l_ref[j] = alpha * l + pv[v_dims:v_dims + 1]
        acc_ref[j] = alpha * acc_ref[j] + pv[:v_dims]
        if diagonal and j == 1:
            finish_tile(qi)

    def advance(pair):
        qi, kb = pair
        end = kb == qi
        nqi, nkb = jnp.where(end, qi + 1, qi), jnp.where(end, 0, kb + 1)
        done = nqi >= n_tiles
        return jnp.where(done, qi, nqi), jnp.where(done, kb, nkb)

    acc_ref[...] = jnp.zeros(acc_ref.shape, F32)
    pair0 = (jnp.int32(0), jnp.int32(0))
    pair1 = advance(pair0)
    for j in range(2):
        score(pair0, 0, j)
        score(pair1, 1, j)

    def half_trip(pa, pb, half):
        pc = advance(pb)
        pd = advance(pc)
        diag_a, diag_b = pa[0] == pa[1], pb[0] == pb[1]
        ra, rb, wc, wd = 2 * half, 2 * half + 1, 2 - 2 * half, 3 - 2 * half

        def run(da, db):
            score(pc, wc, 0)
            score(pc, wc, 1)
            consume(pa, ra, 0, da)
            score(pd, wd, 0)
            consume(pa, ra, 1, da)
            consume(pb, rb, 0, db)
            score(pd, wd, 1)
            consume(pb, rb, 1, db)

        @pl.when(diag_a | diag_b)
        def _():
            pl.when(diag_a)(lambda: run(True, False))
            pl.when(diag_b)(lambda: run(False, True))

        pl.when(jnp.logical_not(diag_a | diag_b))(lambda: run(False, False))
        return pc, pd

    def body(_, carry):
        pa, pb = carry[:2], carry[2:]
        for half in range(2):
            pa, pb = half_trip(pa, pb, half)
        return (*pa, *pb)

    assert n_pairs % 4 == 0
    lax.fori_loop(0, n_pairs // 4, body, (*pair0, *pair1))


def _causal_attention(qt, k, vt, mode, extras=(), lam_init=0.0):
    b, s, _ = k.shape
    t = ATTN_TILE
    assert qt.shape[3] == t and vt.shape[3] == t
    qw = 128 if mode == "diff" else 2 * D_PAD_DIM
    groups = k.shape[2] // qw
    in_specs = [
        pl.BlockSpec((1, s // t, qw, t), lambda bi, h: (bi, 0, h, 0)),
        pl.BlockSpec((1, s, qw), lambda bi, h: (bi, 0, h)),
        pl.BlockSpec((1, s // t, vt.shape[2] // groups, t), lambda bi, h: (bi, 0, h, 0)),
    ] + [_const_spec(e.shape) for e in extras]
    stats = [pltpu.VMEM((2, 1, t), F32)] * 2
    return pl.pallas_call(
        functools.partial(_causal_kernel, mode=mode, lam_init=lam_init, n_tiles=s // t),
        grid=(b, groups),
        in_specs=in_specs,
        out_specs=pl.BlockSpec((1, s, 128), lambda bi, h: (bi, 0, h)),
        out_shape=jax.ShapeDtypeStruct((b, s, groups * 128), BF16),
        scratch_shapes=[
            *[pltpu.VMEM((2, t, t), F32)] * 4,
            *stats, *stats,
            *stats,
            pltpu.VMEM((2, 128 if mode == "diff" else D_V_DIM, t), F32),
        ],
        compiler_params=_cparams(("parallel", "parallel")),
        name=f"causal_attn_{mode}",
    )(qt, k, vt, *extras)


def _cd_out_kernel(x_ref, yc_ref, yd_ref, wc_ref, wd_ref, out_ref):
    out_ref[...] = x_ref[...] + _dot(yc_ref[...], wc_ref[...]) + _dot(yd_ref[...], wd_ref[...])


def _cd_out(x2, yc, yd, w_out):
    t = x2.shape[0]
    row = lambda w: pl.BlockSpec((ROW_TILE, w), lambda i: (i, 0))
    dvw = D_HEADS * D_V_DIM
    return pl.pallas_call(
        _cd_out_kernel,
        grid=(t // ROW_TILE,),
        in_specs=[row(D_MODEL), row(C_WIDTH), row(dvw),
                  _const_spec((C_WIDTH, D_MODEL)), _const_spec((dvw, D_MODEL))],
        out_specs=row(D_MODEL),
        out_shape=jax.ShapeDtypeStruct((t, D_MODEL), F32),
        compiler_params=_cparams(("parallel",)),
        name="cd_out_proj",
    )(x2, yc, yd, w_out[:C_WIDTH].astype(BF16), w_out[C_WIDTH:].astype(BF16))


def kernel(x, mem, positions, norm_mix_g, norm_cross_g, norm_mem_g, cross_wq, cross_wkv, cross_wo,
           norm_mlp_g, mlp_w1, mlp_w2, ab_w_in, ab_w_out, ab_conv_w, ab_conv_b, ab_ln_g, ab_ln_b,
           cd_w_in, cd_w_out, diff_lq1, diff_lk1, diff_lq2, diff_lk2, diff_subln_g, mla_q_norm_g,
           mla_kv_norm_g, mla_w_uq, mla_w_uk, mla_w_uv, final_norm_g):
    b, s, d = x.shape
    t = b * s
    mem_len = mem.shape[1]
    depth = norm_mix_g.shape[0]
    xw = X_HEADS * X_HEAD_DIM
    x2 = x.reshape(t, d)
    for i in range(depth):
        j = i // 2
        if i % 2 == 0:
            *views, glu = _ab_in(x2.reshape(b, s, d), norm_mix_g[i], ab_w_in[j])
            outs, lses = zip(*[_dilated_pattern(view, dil)
                               for view, (_, dil) in zip(views, A_PATTERNS)])
            yb = _conv_module(glu, ab_conv_w[j], ab_conv_b[j], ab_ln_g[j], ab_ln_b[j])
            x2 = _ab_out(x2.reshape(b, s, d), outs, lses, yb, ab_w_out[j]).reshape(t, d)
        else:
            qc, kc, vc, qd, kd, vd = _cd_in(
                x2.reshape(b, s, d), positions, norm_mix_g[i], cd_w_in[j], mla_q_norm_g[j],
                mla_kv_norm_g[j], mla_w_uq[j], mla_w_uk[j], mla_w_uv[j])
            lam_init = 0.8 - 0.6 * math.exp(-0.3 * i)
            vec = lambda a: a.reshape(1, -1)
            yc = _causal_attention(
                qc, kc, vc, "diff",
                extras=(vec(diff_lq1[j]), vec(diff_lk1[j]), vec(diff_lq2[j]), vec(diff_lk2[j]),
                        diff_subln_g[j].reshape(-1, 1)),
                lam_init=lam_init)
            yd = _causal_attention(qd, kd, vd, "mla")
            x2 = _cd_out(x2, yc.reshape(t, C_WIDTH), yd.reshape(t, D_HEADS * D_V_DIM), cd_w_out[j])
        km, vm = _mem_kv(mem.reshape(b * mem_len, d), norm_mem_g[i], cross_wkv[i])
        x2 = _cross_attention(x2.reshape(b, s, d), norm_cross_g[i], cross_wq[i],
                              km.reshape(b, mem_len, xw), vm.reshape(b, mem_len, xw),
                              cross_wo[i]).reshape(t, d)
        x2 = _mlp(x2, norm_mlp_g[i], mlp_w1[i], mlp_w2[i], final_norm_g, i == depth - 1)
    return x2.reshape(b, s, d)
```

```python
import functools
import math

import jax
import jax.numpy as jnp
from jax import lax
from jax.experimental import pallas as pl
from jax.experimental.pallas import tpu as pltpu

F32 = jnp.float32
BF16 = jnp.bfloat16

D_MODEL = 1024
EPS = 1e-6
LOG2E = math.log2(math.e)
NEG = -1e30

A_HEADS = 8
A_HEAD_DIM = 64
A_WIDTH = A_HEADS * A_HEAD_DIM
A_PATTERNS = ((128, 1), (512, 4), (2048, 16))
A_BLK = 128
assert all(window // dilation == A_BLK for window, dilation in A_PATTERNS)
LSE_LANES = 128
assert A_HEADS <= LSE_LANES
B_CHANNELS = 512
B_CONV_WIDTH = 31
B_HALO = 32
C_HEADS = 4
C_HEAD_DIM = 64
C_WIDTH = C_HEADS * 2 * C_HEAD_DIM
D_HEADS = 8
D_NOPE_DIM = 64
D_ROPE_DIM = 32
D_V_DIM = 64
D_Q_RANK = 384
D_KV_RANK = 256
D_PAD_DIM = 128
ROPE_THETA = 10000.0
X_HEADS = 4
X_HEAD_DIM = 128
D_FF = 4 * D_MODEL
FF_CHUNK = 1024

ROW_TILE = 512
ATTN_TILE = ROW_TILE
VMEM_LIMIT = 56 * 1024 * 1024


def _cparams(sem):
    return pltpu.CompilerParams(dimension_semantics=sem, vmem_limit_bytes=VMEM_LIMIT)


def _rms(x, g):
    return x * lax.rsqrt(jnp.mean(x * x, axis=-1, keepdims=True) + EPS) * g


def _dot(a, b):
    return jnp.dot(a, b, preferred_element_type=F32)


def _dot_t(a, b):
    return lax.dot_general(a, b, (((1,), (1,)), ((), ())), preferred_element_type=F32)


def _const_spec(shape):
    nd = len(shape)
    return pl.BlockSpec(shape, lambda *_: (0,) * nd)


def _regather_dilations():
    dils = sorted(d for _, d in A_PATTERNS if d > 1)
    return [e for e in dils if any(d > e and d % e == 0 for d in dils)]


def _ab_in_kernel(x_ref, g_ref, wqkv_ref, wu_ref, wg_ref, *rest):
    n_keep = len(_regather_dilations())
    views, glu_ref = rest[:-2 - n_keep], rest[-2 - n_keep]
    qkv_scr, kept = rest[-1 - n_keep], dict(zip(_regather_dilations(), rest[len(rest) - n_keep:]))
    n = _rms(x_ref[0], g_ref[...]).astype(BF16)
    width = 3 * A_WIDTH
    dense = [v for (_, d), v in zip(A_PATTERNS, views) if d == 1]
    chunk = 256
    assert A_WIDTH % chunk == 0
    for c0 in range(0, width, chunk):
        part = _dot(n, wqkv_ref[:, c0:c0 + chunk])
        if c0 < A_WIDTH:
            part = part * (A_HEAD_DIM ** -0.5 * LOG2E)
        for c in range(c0 // 128, (c0 + chunk) // 128):
            qkv_scr[c] = part[:, c * 128 - c0:(c + 1) * 128 - c0]
        for view_ref in dense:
            view_ref[0, :, c0:c0 + chunk] = part.astype(BF16)
    sources = {1: qkv_scr}
    for d, view_ref in sorted((d, v) for (_, d), v in zip(A_PATTERNS, views) if d > 1):
        e = max(e for e in sources if d % e == 0)
        step, n_d, n_e = d // e, ROW_TILE // d, ROW_TILE // e
        for r in range(d):
            start = (r % e) * n_e + r // e
            for c in range(width // 128):
                rows = sources[e][c, pl.ds(start, n_d, stride=step), :]
                view_ref[0, :, r * width + c * 128:r * width + (c + 1) * 128] = rows.astype(BF16)
                if d in kept:
                    kept[d][c, r * n_d:(r + 1) * n_d, :] = rows
        if d in kept:
            sources[d] = kept[d]
    u = _dot(n, wu_ref[...])
    gate = _dot(n, wg_ref[...])
    glu_ref[0] = u * jax.nn.sigmoid(gate)


def _ab_in(x3, g, w_in):
    b, s, _ = x3.shape
    width = 3 * A_WIDTH
    wqkv = w_in[:, :width].astype(BF16)
    wu = w_in[:, width:width + B_CHANNELS].astype(BF16)
    wg = w_in[:, width + B_CHANNELS:].astype(BF16)
    row = lambda rows, w: pl.BlockSpec((1, rows, w), lambda bi, i: (bi, i, 0))
    dils = [d for _, d in A_PATTERNS]
    return pl.pallas_call(
        _ab_in_kernel,
        grid=(b, s // ROW_TILE),
        in_specs=[
            row(ROW_TILE, D_MODEL),
            _const_spec((1, D_MODEL)),
            _const_spec((D_MODEL, width)),
            _const_spec((D_MODEL, B_CHANNELS)),
            _const_spec((D_MODEL, B_CHANNELS)),
        ],
        out_specs=[row(ROW_TILE // d, d * width) for d in dils] + [row(ROW_TILE, B_CHANNELS)],
        out_shape=[jax.ShapeDtypeStruct((b, s // d, d * width), BF16) for d in dils]
        + [jax.ShapeDtypeStruct((b, s, B_CHANNELS), F32)],
        scratch_shapes=[pltpu.VMEM((width // 128, ROW_TILE, 128), F32)]
        * (1 + len(_regather_dilations())),
        compiler_params=_cparams(("parallel", "parallel")),
        name="ab_in_proj",
    )(x3, g.reshape(1, D_MODEL), wqkv, wu, wg)


def _dilated_kernel(q_ref, kp_ref, kc_ref, vp_ref, vc_ref, o_ref, lse_ref, s_ref, *, rows):
    first_tile = pl.program_id(2) == 0
    qi = lax.broadcasted_iota(jnp.int32, (A_BLK, 2 * A_BLK), 0)
    kj = lax.broadcasted_iota(jnp.int32, (A_BLK, 2 * A_BLK), 1)
    band = ((kj < A_BLK) & (kj >= qi)) | ((kj >= A_BLK) & (kj - A_BLK <= qi))
    lane = lax.broadcasted_iota(jnp.int32, (A_BLK, 128), 1)
    low = lane < A_HEAD_DIM
    pairs = A_WIDTH // 128
    nblk = rows // A_BLK

    def blk(i):
        return slice(i * A_BLK, (i + 1) * A_BLK)

    def prev_and_cur(i, prev_ref, cur_ref, cols):
        prev = prev_ref[0, :, cols] if i == 0 else cur_ref[0, blk(i - 1), cols]
        return jnp.concatenate([prev, cur_ref[0, blk(i), cols]], axis=0)

    def scores(i):
        for pr in range(pairs):
            cols = slice(pr * 128, (pr + 1) * 128)
            q = q_ref[0, blk(i), cols]
            k = prev_and_cur(i, kp_ref, kc_ref, cols)
            zero = jnp.zeros_like(q)
            both = jnp.concatenate([jnp.where(low, q, zero), jnp.where(low, zero, q)], axis=0)
            s = _dot_t(both, k)
            s_ref[i % 2, 2 * pr] = s[:A_BLK]
            s_ref[i % 2, 2 * pr + 1] = s[A_BLK:]

    def consume(i):
        valid = band & ((kj >= A_BLK) | jnp.logical_not(first_tile)) if i == 0 else band
        lse_tile = jnp.zeros((A_BLK, 128), F32)
        for pr in range(pairs):
            cols = slice(pr * 128, (pr + 1) * 128)
            v = prev_and_cur(i, vp_ref, vc_ref, cols)
            ps, ls = [], []
            for half in range(2):
                s = jnp.where(valid, s_ref[i % 2, 2 * pr + half], NEG)
                m = jnp.max(s, axis=-1, keepdims=True)
                p = jnp.exp2(s - m)
                l = jnp.sum(p, axis=-1, keepdims=True)
                ps.append(p.astype(BF16))
                ls.append(l)
                lse_tile = jnp.where(lane == 2 * pr + half, m + jnp.log2(l), lse_tile)
            pv = _dot(jnp.concatenate(ps, axis=0), v)
            out = jnp.where(low, pv[:A_BLK] / ls[0], pv[A_BLK:] / ls[1])
            o_ref[0, blk(i), cols] = out.astype(BF16)
        lse_ref[0, blk(i), :] = lse_tile

    scores(0)
    for i in range(nblk):
        if i + 1 < nblk:
            scores(i + 1)
        consume(i)


def _dilated_pattern(view, dilation):
    b, sub, _ = view.shape
    rows = min(sub, 512)
    assert sub % rows == 0 and rows % A_BLK == 0
    blocks_per_tile = rows // A_BLK
    prev_row = lambda n: jnp.maximum(n * blocks_per_tile - 1, 0)
    cur_spec = lambda part: pl.BlockSpec((1, rows, A_WIDTH),
                                         lambda bi, r, n: (bi, n, 3 * r + part))
    prev_spec = lambda part: pl.BlockSpec((1, A_BLK, A_WIDTH),
                                          lambda bi, r, n: (bi, prev_row(n), 3 * r + part))
    out_spec = lambda w: pl.BlockSpec((1, rows, w), lambda bi, r, n: (bi, n, r))
    return pl.pallas_call(
        functools.partial(_dilated_kernel, rows=rows),
        grid=(b, dilation, sub // rows),
        in_specs=[cur_spec(0), prev_spec(1), cur_spec(1), prev_spec(2), cur_spec(2)],
        out_specs=[out_spec(A_WIDTH), out_spec(LSE_LANES)],
        out_shape=[jax.ShapeDtypeStruct((b, sub, dilation * A_WIDTH), BF16),
                   jax.ShapeDtypeStruct((b, sub, dilation * LSE_LANES), F32)],
        scratch_shapes=[pltpu.VMEM((2, A_HEADS, A_BLK, 2 * A_BLK), F32)],
        compiler_params=_cparams(("parallel", "parallel", "arbitrary")),
        name=f"dilated_attn_d{dilation}",
    )(view, view, view, view, view)


CONV_ROWS = 128
SUBLANES = 8


def _conv_kernel(halo_ref, cur_ref, w_ref, cb_ref, g_ref, b_ref, out_ref, buf_ref, *, rows):
    first_tile = pl.program_id(1) == 0
    halo = halo_ref[0]
    buf_ref[0:B_HALO, :] = jnp.where(first_tile, jnp.zeros_like(halo), halo)
    buf_ref[B_HALO:B_HALO + rows, :] = cur_ref[0]
    lead = B_HALO - (B_CONV_WIDTH - 1)

    def chunk(c, carry):
        base = pl.multiple_of(c * CONV_ROWS, CONV_ROWS)
        span = CONV_ROWS + B_HALO
        accs = []
        for c in range(B_CHANNELS // 128):
            cols = slice(c * 128, (c + 1) * 128)
            win = buf_ref[pl.ds(base, span), cols]
            acc = jnp.zeros((CONV_ROWS, 128), F32)
            for r in range(SUBLANES):
                shifted = pltpu.roll(win, span - r, 0) if r else win
                for tap in range(B_CONV_WIDTH):
                    if (lead + tap) % SUBLANES == r:
                        a = (lead + tap) // SUBLANES * SUBLANES
                        acc = acc + shifted[a:a + CONV_ROWS, :] * w_ref[tap:tap + 1, cols]
            accs.append(acc)
        y = jnp.concatenate(accs, axis=-1) + cb_ref[...]
        mu = jnp.mean(y, axis=-1, keepdims=True)
        yc = y - mu
        var = jnp.mean(yc * yc, axis=-1, keepdims=True)
        z = yc * lax.rsqrt(var + EPS) * g_ref[...] + b_ref[...]
        out_ref[0, pl.ds(base, CONV_ROWS), :] = (z * jax.nn.sigmoid(z)).astype(BF16)
        return carry

    lax.fori_loop(0, rows // CONV_ROWS, chunk, 0)


def _conv_module(glu, conv_w, conv_b, ln_g, ln_b):
    b, s, c = glu.shape
    rows = ROW_TILE
    halo_blocks = rows // B_HALO
    vec = lambda a: a.reshape(1, c)
    return pl.pallas_call(
        functools.partial(_conv_kernel, rows=rows),
        grid=(b, s // rows),
        in_specs=[
            pl.BlockSpec((1, B_HALO, c), lambda bi, n: (bi, jnp.maximum(n * halo_blocks - 1, 0), 0)),
            pl.BlockSpec((1, rows, c), lambda bi, n: (bi, n, 0)),
            _const_spec((B_CONV_WIDTH, c)),
            _const_spec((1, c)), _const_spec((1, c)), _const_spec((1, c)),
        ],
        out_specs=pl.BlockSpec((1, rows, c), lambda bi, n: (bi, n, 0)),
        out_shape=jax.ShapeDtypeStruct((b, s, c), BF16),
        scratch_shapes=[pltpu.VMEM((B_HALO + rows, c), F32)],
        compiler_params=_cparams(("parallel", "arbitrary")),
        name="conformer_conv",
    )(glu, glu, conv_w.reshape(B_CONV_WIDTH, c), vec(conv_b), vec(ln_g), vec(ln_b))


def _finer_dilation(d):
    return max([e for _, e in A_PATTERNS if 1 < e < d and d % e == 0], default=1)


def _reinterleave(src_ref, d, dst_scr, mid_scr, convert):
    tiles = dst_scr.shape[0]
    width = tiles * 128
    e = _finer_dilation(d)
    first = mid_scr if e > 1 else dst_scr
    step, n_d, n_e = d // e, ROW_TILE // d, ROW_TILE // e
    for r in range(d):
        dst = pl.ds((r % e) * n_e + r // e, n_d, stride=step)
        for c in range(tiles):
            first[c, dst, :] = convert(src_ref[0, :, r * width + c * 128:r * width + (c + 1) * 128])
    if e > 1:
        for q in range(e):
            for c in range(tiles):
                dst_scr[c, pl.ds(q, n_e, stride=e), :] = mid_scr[c, q * n_e:(q + 1) * n_e, :]
    return jnp.concatenate([dst_scr[c] for c in range(tiles)], axis=-1)


def _ab_out_kernel(x_ref, *rest):
    n_pat = len(A_PATTERNS)
    o_refs, l_refs = rest[:n_pat], rest[n_pat:2 * n_pat]
    yb_ref, expand_ref, wa_ref, wb_ref, out_ref = rest[2 * n_pat:2 * n_pat + 5]
    scr = list(rest[2 * n_pat + 5:])
    outs, lses = [], []
    for idx, (_, d) in enumerate(A_PATTERNS):
        if d == 1:
            outs.append(o_refs[idx][0].astype(F32))
            lses.append(l_refs[idx][0])
            continue
        o_scr, l_scr = scr.pop(0), scr.pop(0)
        o_mid, l_mid = (scr.pop(0), scr.pop(0)) if _finer_dilation(d) > 1 else (None, None)
        outs.append(_reinterleave(o_refs[idx], d, o_scr, o_mid, lambda v: v.astype(F32)))
        lses.append(_reinterleave(l_refs[idx], d, l_scr, l_mid, lambda v: v))
    m = functools.reduce(jnp.maximum, lses)
    es = [jnp.exp2(l - m) for l in lses]
    inv = 1.0 / sum(es)
    ya = outs[-1]
    for e, o in zip(es[:-1], outs[:-1]):
        w = e * inv
        hi = w.astype(BF16)
        lo = (w - hi.astype(F32)).astype(BF16)
        ya = ya + (_dot(hi, expand_ref[...]) + _dot(lo, expand_ref[...])) * (o - outs[-1])
    out_ref[0] = (x_ref[0] + _dot(ya.astype(BF16), wa_ref[...])
                  + _dot(yb_ref[0], wb_ref[...]))


def _ab_out(x3, outs, lses, yb, w_out):
    b, s, _ = x3.shape
    row = lambda rows, w: pl.BlockSpec((1, rows, w), lambda bi, i: (bi, i, 0))
    o_views = [row(ROW_TILE // d, d * A_WIDTH) for _, d in A_PATTERNS]
    l_views = [row(ROW_TILE // d, d * LSE_LANES) for _, d in A_PATTERNS]
    scratch = []
    for _, d in A_PATTERNS:
        if d > 1:
            scratch += [pltpu.VMEM((A_WIDTH // 128, ROW_TILE, 128), F32),
                        pltpu.VMEM((LSE_LANES // 128, ROW_TILE, 128), F32)] * (
                2 if _finer_dilation(d) > 1 else 1)
    expand = jnp.arange(LSE_LANES)[:, None] == jnp.arange(A_WIDTH)[None, :] // A_HEAD_DIM
    return pl.pallas_call(
        _ab_out_kernel,
        grid=(b, s // ROW_TILE),
        in_specs=[row(ROW_TILE, D_MODEL)] + o_views + l_views + [
            row(ROW_TILE, B_CHANNELS), _const_spec((LSE_LANES, A_WIDTH)),
            _const_spec((A_WIDTH, D_MODEL)), _const_spec((B_CHANNELS, D_MODEL))],
        out_specs=row(ROW_TILE, D_MODEL),
        out_shape=jax.ShapeDtypeStruct((b, s, D_MODEL), F32),
        scratch_shapes=scratch,
        compiler_params=_cparams(("parallel", "parallel")),
        name="ab_out_proj",
    )(x3, *outs, *lses, yb, expand.astype(BF16),
      w_out[:A_WIDTH].astype(BF16), w_out[A_WIDTH:].astype(BF16))


def _mem_kv_kernel(mem_ref, g_ref, wk_ref, wv_ref, k_ref, v_ref):
    n = _rms(mem_ref[...], g_ref[...]).astype(BF16)
    k_ref[...] = _dot(n, wk_ref[...]).astype(BF16)
    v_ref[...] = _dot(n, wv_ref[...]).astype(BF16)


def _mem_kv(mem2, g, wkv):
    rows = mem2.shape[0]
    w = X_HEADS * X_HEAD_DIM
    return pl.pallas_call(
        _mem_kv_kernel,
        grid=(1,),
        in_specs=[_const_spec((rows, D_MODEL)), _const_spec((1, D_MODEL)),
                  _const_spec((D_MODEL, w)), _const_spec((D_MODEL, w))],
        out_specs=[_const_spec((rows, w)), _const_spec((rows, w))],
        out_shape=[jax.ShapeDtypeStruct((rows, w), BF16)] * 2,
        compiler_params=_cparams(("arbitrary",)),
        name="mem_kv_proj",
    )(mem2, g.reshape(1, D_MODEL), wkv[:, :w].astype(BF16), wkv[:, w:].astype(BF16))


def _cross_kernel(x_ref, g_ref, wq_ref, k_ref, v_ref, wo_ref, out_ref):
    x = x_ref[0]
    n = _rms(x, g_ref[...]).astype(BF16)
    q = (_dot(n, wq_ref[...]) * (X_HEAD_DIM ** -0.5)).astype(BF16)
    heads = []
    for h in range(X_HEADS):
        cols = slice(h * X_HEAD_DIM, (h + 1) * X_HEAD_DIM)
        s = _dot_t(q[:, cols], k_ref[0, :, cols])
        m = jnp.max(s, axis=-1, keepdims=True)
        p = jnp.exp(s - m)
        l = jnp.sum(p, axis=-1, keepdims=True)
        heads.append((_dot(p.astype(BF16), v_ref[0, :, cols]) / l).astype(BF16))
    o = jnp.concatenate(heads, axis=-1)
    out_ref[0] = x + _dot(o, wo_ref[...])


def _cross_attention(x3, g, wq, k, v, wo):
    b, s, _ = x3.shape
    m = k.shape[1]
    w = X_HEADS * X_HEAD_DIM
    return pl.pallas_call(
        _cross_kernel,
        grid=(b, s // ROW_TILE),
        in_specs=[
            pl.BlockSpec((1, ROW_TILE, D_MODEL), lambda bi, i: (bi, i, 0)),
            _const_spec((1, D_MODEL)),
            _const_spec((D_MODEL, w)),
            pl.BlockSpec((1, m, w), lambda bi, i: (bi, 0, 0)),
            pl.BlockSpec((1, m, w), lambda bi, i: (bi, 0, 0)),
            _const_spec((w, D_MODEL)),
        ],
        out_specs=pl.BlockSpec((1, ROW_TILE, D_MODEL), lambda bi, i: (bi, i, 0)),
        out_shape=jax.ShapeDtypeStruct((b, s, D_MODEL), F32),
        compiler_params=_cparams(("parallel", "parallel")),
        name="cross_attn",
    )(x3, g.reshape(1, D_MODEL), wq.astype(BF16), k, v, wo.astype(BF16))


def _mlp_kernel(x_ref, g_ref, w1_ref, w2_ref, gf_ref, out_ref, *, final_norm):
    x = x_ref[...]
    n = _rms(x, g_ref[...]).astype(BF16)
    acc = x
    for c in range(D_FF // FF_CHUNK):
        cols = slice(c * FF_CHUNK, (c + 1) * FF_CHUNK)
        h = jnp.maximum(_dot(n, w1_ref[:, cols]), 0.0)
        acc = acc + _dot((h * h).astype(BF16), w2_ref[cols, :])
    if final_norm:
        acc = _rms(acc, gf_ref[...])
    out_ref[...] = acc


def _mlp(x2, g, w1, w2, gf, final_norm):
    t = x2.shape[0]
    return pl.pallas_call(
        functools.partial(_mlp_kernel, final_norm=final_norm),
        grid=(t // ROW_TILE,),
        in_specs=[
            pl.BlockSpec((ROW_TILE, D_MODEL), lambda i: (i, 0)),
            _const_spec((1, D_MODEL)),
            _const_spec((D_MODEL, D_FF)),
            _const_spec((D_FF, D_MODEL)),
            _const_spec((1, D_MODEL)),
        ],
        out_specs=pl.BlockSpec((ROW_TILE, D_MODEL), lambda i: (i, 0)),
        out_shape=jax.ShapeDtypeStruct((t, D_MODEL), F32),
        compiler_params=_cparams(("parallel",)),
        name="mlp_final" if final_norm else "mlp",
    )(x2, g.reshape(1, D_MODEL), w1.astype(BF16), w2.astype(BF16), gf.reshape(1, D_MODEL))


def _rope_rows(xt, cos, sin):
    lo, half = D_NOPE_DIM, D_ROPE_DIM // 2
    x1, x2 = xt[lo:lo + half], xt[lo + half:lo + 2 * half]
    return jnp.concatenate(
        [xt[:lo], x1 * cos - x2 * sin, x2 * cos + x1 * sin, xt[lo + 2 * half:]], axis=0)


ONES_ROWS = 16


def _with_ones_rows(vt, head_rows):
    ones = jnp.ones((ONES_ROWS, vt.shape[1]), F32)
    parts = []
    for r0 in range(0, vt.shape[0], head_rows):
        parts += [vt[r0:r0 + head_rows], ones]
    return jnp.concatenate(parts, axis=0).astype(BF16)


def _cd_in_kernel(x_ref, pos_ref, g_ref, freq_ref, wqkv_ref, wcq_ref, wckv_ref, wkr_ref,
                  qng_ref, kvng_ref, wuq_ref, wuk_ref, wuv_ref,
                  qct_ref, kc_ref, vct_ref, qdt_ref, kd_ref, vdt_ref):
    n = _rms(x_ref[0], g_ref[...]).astype(BF16)
    qc = _dot(n, wqkv_ref[:, :C_WIDTH])
    qct_ref[0, 0] = (qc * (C_HEAD_DIM ** -0.5 * LOG2E)).T.astype(BF16)
    vct_ref[0, 0] = _with_ones_rows(_dot(n, wqkv_ref[:, 2 * C_WIDTH:]).T, 2 * C_HEAD_DIM)

    ang = freq_ref[...] * pos_ref[0, 0].astype(F32)
    cos, sin = jnp.cos(ang), jnp.sin(ang)

    cq = _rms(_dot(n, wcq_ref[...]), qng_ref[...]).astype(BF16)
    q = _dot(cq, wuq_ref[...])
    ckv = _rms(_dot(n, wckv_ref[...]), kvng_ref[...]).astype(BF16)
    k_nope = _dot(ckv, wuk_ref[...])
    vdt_ref[0, 0] = _with_ones_rows(_dot(ckv, wuv_ref[...]).T, D_V_DIM)
    k_rope = _rope_rows(_dot(n, wkr_ref[...]).T, cos, sin).T
    scale = (D_NOPE_DIM + D_ROPE_DIM) ** -0.5 * LOG2E
    for h in range(D_HEADS):
        cols = slice(h * D_PAD_DIM, (h + 1) * D_PAD_DIM)
        qdt_ref[0, 0, cols, :] = (_rope_rows(q[:, cols].T, cos, sin) * scale).astype(BF16)
        kd_ref[0, :, cols] = (k_nope[:, cols] + k_rope).astype(BF16)
    kc_ref[0] = _dot(n, wqkv_ref[:, C_WIDTH:2 * C_WIDTH]).astype(BF16)


def _pad_heads(w, heads, width):
    k = w.shape[0]
    w = w.reshape(k, heads, width)
    return jnp.pad(w, ((0, 0), (0, 0), (0, D_PAD_DIM - width))).reshape(k, heads * D_PAD_DIM)


def _cd_in(x3, positions, g, w_in, q_norm_g, kv_norm_g, w_uq, w_uk, w_uv):
    b, s, _ = x3.shape
    o3 = 3 * C_WIDTH
    o4 = o3 + D_Q_RANK
    o5 = o4 + D_KV_RANK
    wqkv = w_in[:, :o3].astype(BF16)
    wcq = w_in[:, o3:o4].astype(BF16)
    wckv = w_in[:, o4:o5].astype(BF16)
    wkr = jnp.pad(w_in[:, o5:], ((0, 0), (D_NOPE_DIM, D_PAD_DIM - D_NOPE_DIM - D_ROPE_DIM))).astype(BF16)
    wuq = _pad_heads(w_uq, D_HEADS, D_NOPE_DIM + D_ROPE_DIM).astype(BF16)
    wuk = _pad_heads(w_uk, D_HEADS, D_NOPE_DIM).astype(BF16)
    half = D_ROPE_DIM // 2
    freq = (ROPE_THETA ** (-jnp.arange(half, dtype=F32) / half)).reshape(half, 1)
    dw = D_HEADS * D_PAD_DIM
    vc_rows = C_HEADS * (2 * C_HEAD_DIM + ONES_ROWS)
    vd_rows = D_HEADS * (D_V_DIM + ONES_ROWS)
    nt = s // ROW_TILE
    row = lambda w: pl.BlockSpec((1, ROW_TILE, w), lambda bi, i: (bi, i, 0))
    colmajor = lambda w: pl.BlockSpec((1, 1, w, ROW_TILE), lambda bi, i: (bi, i, 0, 0))
    return pl.pallas_call(
        _cd_in_kernel,
        grid=(b, s // ROW_TILE),
        in_specs=[
            row(D_MODEL), colmajor(1), _const_spec((1, D_MODEL)), _const_spec((half, 1)),
            _const_spec((D_MODEL, o3)), _const_spec((D_MODEL, D_Q_RANK)),
            _const_spec((D_MODEL, D_KV_RANK)), _const_spec((D_MODEL, D_PAD_DIM)),
            _const_spec((1, D_Q_RANK)), _const_spec((1, D_KV_RANK)),
            _const_spec((D_Q_RANK, dw)), _const_spec((D_KV_RANK, dw)),
            _const_spec((D_KV_RANK, D_HEADS * D_V_DIM)),
        ],
        out_specs=[colmajor(C_WIDTH), row(C_WIDTH), colmajor(vc_rows), colmajor(dw), row(dw),
                   colmajor(vd_rows)],
        out_shape=[
            jax.ShapeDtypeStruct((b, nt, C_WIDTH, ROW_TILE), BF16),
            jax.ShapeDtypeStruct((b, s, C_WIDTH), BF16),
            jax.ShapeDtypeStruct((b, nt, vc_rows, ROW_TILE), BF16),
            jax.ShapeDtypeStruct((b, nt, dw, ROW_TILE), BF16),
            jax.ShapeDtypeStruct((b, s, dw), BF16),
            jax.ShapeDtypeStruct((b, nt, vd_rows, ROW_TILE), BF16),
        ],
        compiler_params=_cparams(("parallel", "parallel")),
        name="cd_in_proj",
    )(x3, positions.reshape(b, nt, 1, ROW_TILE), g.reshape(1, D_MODEL), freq, wqkv, wcq, wckv, wkr,
      q_norm_g.reshape(1, D_Q_RANK), kv_norm_g.reshape(1, D_KV_RANK), wuq, wuk,
      w_uv.astype(BF16))


def _causal_kernel(qt_ref, k_ref, vt_ref, *rest, mode, lam_init, n_tiles):
    if mode == "diff":
        lq1_ref, lk1_ref, lq2_ref, lk2_ref, g_ref, out_ref = rest[:6]
    else:
        out_ref = rest[0]
    s_bufs, mx_bufs = rest[-11:-7], rest[-7:-3]
    m_ref, l_ref, acc_ref = rest[-3:]
    t = ATTN_TILE
    v_dims = 2 * C_HEAD_DIM if mode == "diff" else D_V_DIM
    v_rows = v_dims + ONES_ROWS
    n_pairs = n_tiles * (n_tiles + 1) // 2
    assert n_tiles >= 2 and n_pairs % 2 == 0
    feat = lax.broadcasted_iota(jnp.int32, (128, t), 0)

    def score(pair, buf, j):
        qi, kb = pair
        rows = pl.ds(pl.multiple_of(kb * t, t), t)
        if mode == "diff":
            qt = qt_ref[0, qi]
            mine = (feat < C_HEAD_DIM) if j == 0 else (feat >= C_HEAD_DIM)
            q, k = jnp.where(mine, qt, jnp.zeros_like(qt)), k_ref[0, rows, :]
        else:
            cols = slice(j * D_PAD_DIM, (j + 1) * D_PAD_DIM)
            q, k = qt_ref[0, qi, cols, :], k_ref[0, rows, cols]
        s = _dot(k, q)
        s_bufs[buf][j] = s
        mx_bufs[buf][j] = jnp.max(s, axis=0, keepdims=True)

    def finish_tile(qi):
        a1 = acc_ref[0] / l_ref[0]
        a2 = acc_ref[1] / l_ref[1]
        if mode == "diff":
            lam = (jnp.exp(jnp.sum(lq1_ref[...] * lk1_ref[...], axis=-1, keepdims=True))
                   - jnp.exp(jnp.sum(lq2_ref[...] * lk2_ref[...], axis=-1, keepdims=True))
                   + lam_init)
            d = a1 - lam * a2
            out = (d * lax.rsqrt(jnp.mean(d * d, axis=0, keepdims=True) + EPS) * g_ref[...]
                   * (1.0 - lam_init))
        else:
            out = jnp.concatenate([a1, a2], axis=0)
        out_ref[0, pl.ds(pl.multiple_of(qi * t, t), t), :] = out.T.astype(BF16)

    def consume(pair, buf, j, diagonal):
        qi, kb = pair
        first = kb == 0
        s = s_bufs[buf][j]
        if diagonal:
            kr = lax.broadcasted_iota(jnp.int32, (t, t), 0)
            qc = lax.broadcasted_iota(jnp.int32, (t, t), 1)
            s = jnp.where(kr <= qc, s, NEG)
            block_max = jnp.max(s, axis=0, keepdims=True)
        else:
            block_max = mx_bufs[buf][j]
        m = jnp.where(first, NEG, m_ref[j])
        l = jnp.where(first, 0.0, l_ref[j])
        m_new = jnp.maximum(m, block_max)
        alpha = jnp.exp2(m - m_new)
        p = jnp.exp2((s - m_new).astype(BF16))
        m_ref[j] = m_new
        if mode == "diff":
            vt = vt_ref[0, kb]
        else:
            vt = vt_ref[0, kb, j * v_rows:(j + 1) * v_rows, :]
        pv = _dot(vt, p)
        l_ref[j] = alpha * l + pv[v_dims:v_dims + 1]
        acc_ref[j] = alpha * acc_ref[j] + pv[:v_dims]
        if diagonal and j == 1:
            finish_tile(qi)

    def advance(pair):
        qi, kb = pair
        end = kb == qi
        nqi, nkb = jnp.where(end, qi + 1, qi), jnp.where(end, 0, kb + 1)
        done = nqi >= n_tiles
        return jnp.where(done, qi, nqi), jnp.where(done, kb, nkb)

    acc_ref[...] = jnp.zeros(acc_ref.shape, F32)
    pair0 = (jnp.int32(0), jnp.int32(0))
    pair1 = advance(pair0)
    for j in range(2):
        score(pair0, 0, j)
        score(pair1, 1, j)

    def half_trip(pa, pb, half):
        pc = advance(pb)
        pd = advance(pc)
        diag_a, diag_b = pa[0] == pa[1], pb[0] == pb[1]
        ra, rb, wc, wd = 2 * half, 2 * half + 1, 2 - 2 * half, 3 - 2 * half

        def run(da, db):
            score(pc, wc, 0)
            score(pc, wc, 1)
            consume(pa, ra, 0, da)
            score(pd, wd, 0)
            consume(pa, ra, 1, da)
            consume(pb, rb, 0, db)
            score(pd, wd, 1)
            consume(pb, rb, 1, db)

        @pl.when(diag_a | diag_b)
        def _():
            pl.when(diag_a)(lambda: run(True, False))
            pl.when(diag_b)(lambda: run(False, True))

        pl.when(jnp.logical_not(diag_a | diag_b))(lambda: run(False, False))
        return pc, pd

    def body(_, carry):
        pa, pb = carry[:2], carry[2:]
        for half in range(2):
            pa, pb = half_trip(pa, pb, half)
        return (*pa, *pb)

    assert n_pairs % 4 == 0
    lax.fori_loop(0, n_pairs // 4, body, (*pair0, *pair1))


def _causal_attention(qt, k, vt, mode, extras=(), lam_init=0.0):
    b, s, _ = k.shape
    t = ATTN_TILE
    assert qt.shape[3] == t and vt.shape[3] == t
    qw = 128 if mode == "diff" else 2 * D_PAD_DIM
    groups = k.shape[2] // qw
    in_specs = [
        pl.BlockSpec((1, s // t, qw, t), lambda bi, h: (bi, 0, h, 0)),
        pl.BlockSpec((1, s, qw), lambda bi, h: (bi, 0, h)),
        pl.BlockSpec((1, s // t, vt.shape[2] // groups, t), lambda bi, h: (bi, 0, h, 0)),
    ] + [_const_spec(e.shape) for e in extras]
    stats = [pltpu.VMEM((2, 1, t), F32)] * 2
    return pl.pallas_call(
        functools.partial(_causal_kernel, mode=mode, lam_init=lam_init, n_tiles=s // t),
        grid=(b, groups),
        in_specs=in_specs,
        out_specs=pl.BlockSpec((1, s, 128), lambda bi, h: (bi, 0, h)),
        out_shape=jax.ShapeDtypeStruct((b, s, groups * 128), BF16),
        scratch_shapes=[
            *[pltpu.VMEM((2, t, t), F32)] * 4,
            *stats, *stats,
            *stats,
            pltpu.VMEM((2, 128 if mode == "diff" else D_V_DIM, t), F32),
        ],
        compiler_params=_cparams(("parallel", "parallel")),
        name=f"causal_attn_{mode}",
    )(qt, k, vt, *extras)


def _cd_out_kernel(x_ref, yc_ref, yd_ref, wc_ref, wd_ref, out_ref):
    out_ref[...] = x_ref[...] + _dot(yc_ref[...], wc_ref[...]) + _dot(yd_ref[...], wd_ref[...])


def _cd_out(x2, yc, yd, w_out):
    t = x2.shape[0]
    row = lambda w: pl.BlockSpec((ROW_TILE, w), lambda i: (i, 0))
    dvw = D_HEADS * D_V_DIM
    return pl.pallas_call(
        _cd_out_kernel,
        grid=(t // ROW_TILE,),
        in_specs=[row(D_MODEL), row(C_WIDTH), row(dvw),
                  _const_spec((C_WIDTH, D_MODEL)), _const_spec((dvw, D_MODEL))],
        out_specs=row(D_MODEL),
        out_shape=jax.ShapeDtypeStruct((t, D_MODEL), F32),
        compiler_params=_cparams(("parallel",)),
        name="cd_out_proj",
    )(x2, yc, yd, w_out[:C_WIDTH].astype(BF16), w_out[C_WIDTH:].astype(BF16))


def kernel(x, mem, positions, norm_mix_g, norm_cross_g, norm_mem_g, cross_wq, cross_wkv, cross_wo,
           norm_mlp_g, mlp_w1, mlp_w2, ab_w_in, ab_w_out, ab_conv_w, ab_conv_b, ab_ln_g, ab_ln_b,
           cd_w_in, cd_w_out, diff_lq1, diff_lk1, diff_lq2, diff_lk2, diff_subln_g, mla_q_norm_g,
           mla_kv_norm_g, mla_w_uq, mla_w_uk, mla_w_uv, final_norm_g):
    b, s, d = x.shape
    t = b * s
    mem_len = mem.shape[1]
    depth = norm_mix_g.shape[0]
    xw = X_HEADS * X_HEAD_DIM
    x2 = x.reshape(t, d)
    for i in range(depth):
        j = i // 2
        if i % 2 == 0:
            *views, glu = _ab_in(x2.reshape(b, s, d), norm_mix_g[i], ab_w_in[j])
            outs, lses = zip(*[_dilated_pattern(view, dil)
                               for view, (_, dil) in zip(views, A_PATTERNS)])
            yb = _conv_module(glu, ab_conv_w[j], ab_conv_b[j], ab_ln_g[j], ab_ln_b[j])
            x2 = _ab_out(x2.reshape(b, s, d), outs, lses, yb, ab_w_out[j]).reshape(t, d)
        else:
            qc, kc, vc, qd, kd, vd = _cd_in(
                x2.reshape(b, s, d), positions, norm_mix_g[i], cd_w_in[j], mla_q_norm_g[j],
                mla_kv_norm_g[j], mla_w_uq[j], mla_w_uk[j], mla_w_uv[j])
            lam_init = 0.8 - 0.6 * math.exp(-0.3 * i)
            vec = lambda a: a.reshape(1, -1)
            yc = _causal_attention(
                qc, kc, vc, "diff",
                extras=(vec(diff_lq1[j]), vec(diff_lk1[j]), vec(diff_lq2[j]), vec(diff_lk2[j]),
                        diff_subln_g[j].reshape(-1, 1)),
                lam_init=lam_init)
            yd = _causal_attention(qd, kd, vd, "mla")
            x2 = _cd_out(x2, yc.reshape(t, C_WIDTH), yd.reshape(t, D_HEADS * D_V_DIM), cd_w_out[j])
        km, vm = _mem_kv(mem.reshape(b * mem_len, d), norm_mem_g[i], cross_wkv[i])
        x2 = _cross_attention(x2.reshape(b, s, d), norm_cross_g[i], cross_wq[i],
                              km.reshape(b, mem_len, xw), vm.reshape(b, mem_len, xw),
                              cross_wo[i]).reshape(t, d)
        x2 = _mlp(x2, norm_mlp_g[i], mlp_w1[i], mlp_w2[i], final_norm_g, i == depth - 1)
    return x2.reshape(b, s, d)
```

```python
import functools
import math

import jax
import jax.numpy as jnp
from jax import lax
from jax.experimental import pallas as pl
from jax.experimental.pallas import tpu as pltpu

F32 = jnp.float32
BF16 = jnp.bfloat16

D_MODEL = 1024
EPS = 1e-6
LOG2E = math.log2(math.e)
NEG = -1e30

A_HEADS = 8
A_HEAD_DIM = 64
A_WIDTH = A_HEADS * A_HEAD_DIM
A_PATTERNS = ((128, 1), (512, 4), (2048, 16))
A_BLK = 128
assert all(window // dilation == A_BLK for window, dilation in A_PATTERNS)
LSE_LANES = 128
assert A_HEADS <= LSE_LANES
B_CHANNELS = 512
B_CONV_WIDTH = 31
B_HALO = 32
C_HEADS = 4
C_HEAD_DIM = 64
C_WIDTH = C_HEADS * 2 * C_HEAD_DIM
D_HEADS = 8
D_NOPE_DIM = 64
D_ROPE_DIM = 32
D_V_DIM = 64
D_Q_RANK = 384
D_KV_RANK = 256
D_PAD_DIM = 128
ROPE_THETA = 10000.0
X_HEADS = 4
X_HEAD_DIM = 128
D_FF = 4 * D_MODEL
FF_CHUNK = 1024

ROW_TILE = 512
WIDE_ROW_TILE = 1024
ATTN_TILE = ROW_TILE
VMEM_LIMIT = 56 * 1024 * 1024


def _cparams(sem):
    return pltpu.CompilerParams(dimension_semantics=sem, vmem_limit_bytes=VMEM_LIMIT)


def _rms(x, g):
    return x * lax.rsqrt(jnp.mean(x * x, axis=-1, keepdims=True) + EPS) * g


def _dot(a, b):
    return jnp.dot(a, b, preferred_element_type=F32)


def _dot_t(a, b):
    return lax.dot_general(a, b, (((1,), (1,)), ((), ())), preferred_element_type=F32)


def _const_spec(shape):
    nd = len(shape)
    return pl.BlockSpec(shape, lambda *_: (0,) * nd)


def _regather_dilations():
    dils = sorted(d for _, d in A_PATTERNS if d > 1)
    return [e for e in dils if any(d > e and d % e == 0 for d in dils)]


def _ab_in_kernel(x_ref, g_ref, wqkv_ref, wu_ref, wg_ref, *rest):
    n_keep = len(_regather_dilations())
    views, glu_ref = rest[:-2 - n_keep], rest[-2 - n_keep]
    qkv_scr, kept = rest[-1 - n_keep], dict(zip(_regather_dilations(), rest[len(rest) - n_keep:]))
    n = _rms(x_ref[0], g_ref[...]).astype(BF16)
    width = 3 * A_WIDTH
    dense = [v for (_, d), v in zip(A_PATTERNS, views) if d == 1]
    chunk = 256
    assert A_WIDTH % chunk == 0
    for c0 in range(0, width, chunk):
        part = _dot(n, wqkv_ref[:, c0:c0 + chunk])
        if c0 < A_WIDTH:
            part = part * (A_HEAD_DIM ** -0.5 * LOG2E)
        for c in range(c0 // 128, (c0 + chunk) // 128):
            qkv_scr[c] = part[:, c * 128 - c0:(c + 1) * 128 - c0]
        for view_ref in dense:
            view_ref[0, :, c0:c0 + chunk] = part.astype(BF16)
    sources = {1: qkv_scr}
    for d, view_ref in sorted((d, v) for (_, d), v in zip(A_PATTERNS, views) if d > 1):
        e = max(e for e in sources if d % e == 0)
        step, n_d, n_e = d // e, ROW_TILE // d, ROW_TILE // e
        for r in range(d):
            start = (r % e) * n_e + r // e
            for c in range(width // 128):
                rows = sources[e][c, pl.ds(start, n_d, stride=step), :]
                view_ref[0, :, r * width + c * 128:r * width + (c + 1) * 128] = rows.astype(BF16)
                if d in kept:
                    kept[d][c, r * n_d:(r + 1) * n_d, :] = rows
        if d in kept:
            sources[d] = kept[d]
    u = _dot(n, wu_ref[...])
    gate = _dot(n, wg_ref[...])
    glu_ref[0] = u * jax.nn.sigmoid(gate)


def _ab_in(x3, g, w_in):
    b, s, _ = x3.shape
    width = 3 * A_WIDTH
    wqkv = w_in[:, :width].astype(BF16)
    wu = w_in[:, width:width + B_CHANNELS].astype(BF16)
    wg = w_in[:, width + B_CHANNELS:].astype(BF16)
    row = lambda rows, w: pl.BlockSpec((1, rows, w), lambda bi, i: (bi, i, 0))
    dils = [d for _, d in A_PATTERNS]
    return pl.pallas_call(
        _ab_in_kernel,
        grid=(b, s // ROW_TILE),
        in_specs=[
            row(ROW_TILE, D_MODEL),
            _const_spec((1, D_MODEL)),
            _const_spec((D_MODEL, width)),
            _const_spec((D_MODEL, B_CHANNELS)),
            _const_spec((D_MODEL, B_CHANNELS)),
        ],
        out_specs=[row(ROW_TILE // d, d * width) for d in dils] + [row(ROW_TILE, B_CHANNELS)],
        out_shape=[jax.ShapeDtypeStruct((b, s // d, d * width), BF16) for d in dils]
        + [jax.ShapeDtypeStruct((b, s, B_CHANNELS), F32)],
        scratch_shapes=[pltpu.VMEM((width // 128, ROW_TILE, 128), F32)]
        * (1 + len(_regather_dilations())),
        compiler_params=_cparams(("parallel", "parallel")),
        name="ab_in_proj",
    )(x3, g.reshape(1, D_MODEL), wqkv, wu, wg)


def _dilated_kernel(q_ref, kp_ref, kc_ref, vp_ref, vc_ref, o_ref, lse_ref, s_ref, *, rows):
    first_tile = pl.program_id(2) == 0
    qi = lax.broadcasted_iota(jnp.int32, (A_BLK, 2 * A_BLK), 0)
    kj = lax.broadcasted_iota(jnp.int32, (A_BLK, 2 * A_BLK), 1)
    band = ((kj < A_BLK) & (kj >= qi)) | ((kj >= A_BLK) & (kj - A_BLK <= qi))
    lane = lax.broadcasted_iota(jnp.int32, (A_BLK, 128), 1)
    low = lane < A_HEAD_DIM
    pairs = A_WIDTH // 128
    nblk = rows // A_BLK

    def blk(i):
        return slice(i * A_BLK, (i + 1) * A_BLK)

    def prev_and_cur(i, prev_ref, cur_ref, cols):
        prev = prev_ref[0, :, cols] if i == 0 else cur_ref[0, blk(i - 1), cols]
        return jnp.concatenate([prev, cur_ref[0, blk(i), cols]], axis=0)

    def scores(i):
        for pr in range(pairs):
            cols = slice(pr * 128, (pr + 1) * 128)
            q = q_ref[0, blk(i), cols]
            k = prev_and_cur(i, kp_ref, kc_ref, cols)
            zero = jnp.zeros_like(q)
            both = jnp.concatenate([jnp.where(low, q, zero), jnp.where(low, zero, q)], axis=0)
            s = _dot_t(both, k)
            s_ref[i % 2, 2 * pr] = s[:A_BLK]
            s_ref[i % 2, 2 * pr + 1] = s[A_BLK:]

    def consume(i):
        valid = band & ((kj >= A_BLK) | jnp.logical_not(first_tile)) if i == 0 else band
        lse_tile = jnp.zeros((A_BLK, 128), F32)
        for pr in range(pairs):
            cols = slice(pr * 128, (pr + 1) * 128)
            v = prev_and_cur(i, vp_ref, vc_ref, cols)
            ps, ls = [], []
            for half in range(2):
                s = jnp.where(valid, s_ref[i % 2, 2 * pr + half], NEG)
                m = jnp.max(s, axis=-1, keepdims=True)
                p = jnp.exp2(s - m)
                l = jnp.sum(p, axis=-1, keepdims=True)
                ps.append(p.astype(BF16))
                ls.append(l)
                lse_tile = jnp.where(lane == 2 * pr + half, m + jnp.log2(l), lse_tile)
            pv = _dot(jnp.concatenate(ps, axis=0), v)
            out = jnp.where(low, pv[:A_BLK] / ls[0], pv[A_BLK:] / ls[1])
            o_ref[0, blk(i), cols] = out.astype(BF16)
        lse_ref[0, blk(i), :] = lse_tile

    scores(0)
    for i in range(nblk):
        if i + 1 < nblk:
            scores(i + 1)
        consume(i)


def _dilated_pattern(view, dilation):
    b, sub, _ = view.shape
    rows = min(sub, 512)
    assert sub % rows == 0 and rows % A_BLK == 0
    blocks_per_tile = rows // A_BLK
    prev_row = lambda n: jnp.maximum(n * blocks_per_tile - 1, 0)
    cur_spec = lambda part: pl.BlockSpec((1, rows, A_WIDTH),
                                         lambda bi, r, n: (bi, n, 3 * r + part))
    prev_spec = lambda part: pl.BlockSpec((1, A_BLK, A_WIDTH),
                                          lambda bi, r, n: (bi, prev_row(n), 3 * r + part))
    out_spec = lambda w: pl.BlockSpec((1, rows, w), lambda bi, r, n: (bi, n, r))
    return pl.pallas_call(
        functools.partial(_dilated_kernel, rows=rows),
        grid=(b, dilation, sub // rows),
        in_specs=[cur_spec(0), prev_spec(1), cur_spec(1), prev_spec(2), cur_spec(2)],
        out_specs=[out_spec(A_WIDTH), out_spec(LSE_LANES)],
        out_shape=[jax.ShapeDtypeStruct((b, sub, dilation * A_WIDTH), BF16),
                   jax.ShapeDtypeStruct((b, sub, dilation * LSE_LANES), F32)],
        scratch_shapes=[pltpu.VMEM((2, A_HEADS, A_BLK, 2 * A_BLK), F32)],
        compiler_params=_cparams(("parallel", "parallel", "arbitrary")),
        name=f"dilated_attn_d{dilation}",
    )(view, view, view, view, view)


CONV_ROWS = 128
SUBLANES = 8


def _conv_kernel(halo_ref, cur_ref, w_ref, cb_ref, g_ref, b_ref, out_ref, buf_ref, *, rows):
    first_tile = pl.program_id(1) == 0
    halo = halo_ref[0]
    buf_ref[0:B_HALO, :] = jnp.where(first_tile, jnp.zeros_like(halo), halo)
    buf_ref[B_HALO:B_HALO + rows, :] = cur_ref[0]
    lead = B_HALO - (B_CONV_WIDTH - 1)

    def chunk(c, carry):
        base = pl.multiple_of(c * CONV_ROWS, CONV_ROWS)
        span = CONV_ROWS + B_HALO
        accs = []
        for c in range(B_CHANNELS // 128):
            cols = slice(c * 128, (c + 1) * 128)
            win = buf_ref[pl.ds(base, span), cols]
            acc = jnp.zeros((CONV_ROWS, 128), F32)
            for r in range(SUBLANES):
                shifted = pltpu.roll(win, span - r, 0) if r else win
                for tap in range(B_CONV_WIDTH):
                    if (lead + tap) % SUBLANES == r:
                        a = (lead + tap) // SUBLANES * SUBLANES
                        acc = acc + shifted[a:a + CONV_ROWS, :] * w_ref[tap:tap + 1, cols]
            accs.append(acc)
        y = jnp.concatenate(accs, axis=-1) + cb_ref[...]
        mu = jnp.mean(y, axis=-1, keepdims=True)
        yc = y - mu
        var = jnp.mean(yc * yc, axis=-1, keepdims=True)
        z = yc * lax.rsqrt(var + EPS) * g_ref[...] + b_ref[...]
        out_ref[0, pl.ds(base, CONV_ROWS), :] = (z * jax.nn.sigmoid(z)).astype(BF16)
        return carry

    lax.fori_loop(0, rows // CONV_ROWS, chunk, 0)


def _conv_module(glu, conv_w, conv_b, ln_g, ln_b):
    b, s, c = glu.shape
    rows = WIDE_ROW_TILE
    halo_blocks = rows // B_HALO
    vec = lambda a: a.reshape(1, c)
    return pl.pallas_call(
        functools.partial(_conv_kernel, rows=rows),
        grid=(b, s // rows),
        in_specs=[
            pl.BlockSpec((1, B_HALO, c), lambda bi, n: (bi, jnp.maximum(n * halo_blocks - 1, 0), 0)),
            pl.BlockSpec((1, rows, c), lambda bi, n: (bi, n, 0)),
            _const_spec((B_CONV_WIDTH, c)),
            _const_spec((1, c)), _const_spec((1, c)), _const_spec((1, c)),
        ],
        out_specs=pl.BlockSpec((1, rows, c), lambda bi, n: (bi, n, 0)),
        out_shape=jax.ShapeDtypeStruct((b, s, c), BF16),
        scratch_shapes=[pltpu.VMEM((B_HALO + rows, c), F32)],
        compiler_params=_cparams(("parallel", "arbitrary")),
        name="conformer_conv",
    )(glu, glu, conv_w.reshape(B_CONV_WIDTH, c), vec(conv_b), vec(ln_g), vec(ln_b))


def _finer_dilation(d):
    return max([e for _, e in A_PATTERNS if 1 < e < d and d % e == 0], default=1)


def _reinterleave(src_ref, d, dst_scr, mid_scr, convert):
    tiles = dst_scr.shape[0]
    width = tiles * 128
    e = _finer_dilation(d)
    first = mid_scr if e > 1 else dst_scr
    step, n_d, n_e = d // e, ROW_TILE // d, ROW_TILE // e
    for r in range(d):
        dst = pl.ds((r % e) * n_e + r // e, n_d, stride=step)
        for c in range(tiles):
            first[c, dst, :] = convert(src_ref[0, :, r * width + c * 128:r * width + (c + 1) * 128])
    if e > 1:
        for q in range(e):
            for c in range(tiles):
                dst_scr[c, pl.ds(q, n_e, stride=e), :] = mid_scr[c, q * n_e:(q + 1) * n_e, :]
    return jnp.concatenate([dst_scr[c] for c in range(tiles)], axis=-1)


def _ab_out_kernel(x_ref, *rest):
    n_pat = len(A_PATTERNS)
    o_refs, l_refs = rest[:n_pat], rest[n_pat:2 * n_pat]
    yb_ref, expand_ref, wa_ref, wb_ref, out_ref = rest[2 * n_pat:2 * n_pat + 5]
    scr = list(rest[2 * n_pat + 5:])
    outs, lses = [], []
    for idx, (_, d) in enumerate(A_PATTERNS):
        if d == 1:
            outs.append(o_refs[idx][0].astype(F32))
            lses.append(l_refs[idx][0])
            continue
        o_scr, l_scr = scr.pop(0), scr.pop(0)
        o_mid, l_mid = (scr.pop(0), scr.pop(0)) if _finer_dilation(d) > 1 else (None, None)
        outs.append(_reinterleave(o_refs[idx], d, o_scr, o_mid, lambda v: v.astype(F32)))
        lses.append(_reinterleave(l_refs[idx], d, l_scr, l_mid, lambda v: v))
    m = functools.reduce(jnp.maximum, lses)
    es = [jnp.exp2(l - m) for l in lses]
    inv = 1.0 / sum(es)
    ya = outs[-1]
    for e, o in zip(es[:-1], outs[:-1]):
        w = e * inv
        hi = w.astype(BF16)
        lo = (w - hi.astype(F32)).astype(BF16)
        ya = ya + (_dot(hi, expand_ref[...]) + _dot(lo, expand_ref[...])) * (o - outs[-1])
    out_ref[0] = (x_ref[0] + _dot(ya.astype(BF16), wa_ref[...])
                  + _dot(yb_ref[0], wb_ref[...]))


def _ab_out(x3, outs, lses, yb, w_out):
    b, s, _ = x3.shape
    row = lambda rows, w: pl.BlockSpec((1, rows, w), lambda bi, i: (bi, i, 0))
    o_views = [row(ROW_TILE // d, d * A_WIDTH) for _, d in A_PATTERNS]
    l_views = [row(ROW_TILE // d, d * LSE_LANES) for _, d in A_PATTERNS]
    scratch = []
    for _, d in A_PATTERNS:
        if d > 1:
            scratch += [pltpu.VMEM((A_WIDTH // 128, ROW_TILE, 128), F32),
                        pltpu.VMEM((LSE_LANES // 128, ROW_TILE, 128), F32)] * (
                2 if _finer_dilation(d) > 1 else 1)
    expand = jnp.arange(LSE_LANES)[:, None] == jnp.arange(A_WIDTH)[None, :] // A_HEAD_DIM
    return pl.pallas_call(
        _ab_out_kernel,
        grid=(b, s // ROW_TILE),
        in_specs=[row(ROW_TILE, D_MODEL)] + o_views + l_views + [
            row(ROW_TILE, B_CHANNELS), _const_spec((LSE_LANES, A_WIDTH)),
            _const_spec((A_WIDTH, D_MODEL)), _const_spec((B_CHANNELS, D_MODEL))],
        out_specs=row(ROW_TILE, D_MODEL),
        out_shape=jax.ShapeDtypeStruct((b, s, D_MODEL), F32),
        scratch_shapes=scratch,
        compiler_params=_cparams(("parallel", "parallel")),
        name="ab_out_proj",
    )(x3, *outs, *lses, yb, expand.astype(BF16),
      w_out[:A_WIDTH].astype(BF16), w_out[A_WIDTH:].astype(BF16))


def _mem_kv_kernel(mem_ref, g_ref, wk_ref, wv_ref, k_ref, v_ref):
    n = _rms(mem_ref[...], g_ref[...]).astype(BF16)
    k_ref[...] = _dot(n, wk_ref[...]).astype(BF16)
    v_ref[...] = _dot(n, wv_ref[...]).astype(BF16)


def _mem_kv(mem2, g, wkv):
    rows = mem2.shape[0]
    w = X_HEADS * X_HEAD_DIM
    return pl.pallas_call(
        _mem_kv_kernel,
        grid=(1,),
        in_specs=[_const_spec((rows, D_MODEL)), _const_spec((1, D_MODEL)),
                  _const_spec((D_MODEL, w)), _const_spec((D_MODEL, w))],
        out_specs=[_const_spec((rows, w)), _const_spec((rows, w))],
        out_shape=[jax.ShapeDtypeStruct((rows, w), BF16)] * 2,
        compiler_params=_cparams(("arbitrary",)),
        name="mem_kv_proj",
    )(mem2, g.reshape(1, D_MODEL), wkv[:, :w].astype(BF16), wkv[:, w:].astype(BF16))


def _cross_kernel(x_ref, g_ref, wq_ref, k_ref, v_ref, wo_ref, out_ref):
    x = x_ref[0]
    n = _rms(x, g_ref[...]).astype(BF16)
    q = (_dot(n, wq_ref[...]) * (X_HEAD_DIM ** -0.5)).astype(BF16)
    heads = []
    for h in range(X_HEADS):
        cols = slice(h * X_HEAD_DIM, (h + 1) * X_HEAD_DIM)
        s = _dot_t(q[:, cols], k_ref[0, :, cols])
        m = jnp.max(s, axis=-1, keepdims=True)
        p = jnp.exp(s - m)
        l = jnp.sum(p, axis=-1, keepdims=True)
        heads.append((_dot(p.astype(BF16), v_ref[0, :, cols]) / l).astype(BF16))
    o = jnp.concatenate(heads, axis=-1)
    out_ref[0] = x + _dot(o, wo_ref[...])


def _cross_attention(x3, g, wq, k, v, wo):
    b, s, _ = x3.shape
    m = k.shape[1]
    w = X_HEADS * X_HEAD_DIM
    rows = WIDE_ROW_TILE
    return pl.pallas_call(
        _cross_kernel,
        grid=(b, s // rows),
        in_specs=[
            pl.BlockSpec((1, rows, D_MODEL), lambda bi, i: (bi, i, 0)),
            _const_spec((1, D_MODEL)),
            _const_spec((D_MODEL, w)),
            pl.BlockSpec((1, m, w), lambda bi, i: (bi, 0, 0)),
            pl.BlockSpec((1, m, w), lambda bi, i: (bi, 0, 0)),
            _const_spec((w, D_MODEL)),
        ],
        out_specs=pl.BlockSpec((1, rows, D_MODEL), lambda bi, i: (bi, i, 0)),
        out_shape=jax.ShapeDtypeStruct((b, s, D_MODEL), F32),
        compiler_params=_cparams(("parallel", "parallel")),
        name="cross_attn",
    )(x3, g.reshape(1, D_MODEL), wq.astype(BF16), k, v, wo.astype(BF16))


def _mlp_kernel(x_ref, g_ref, w1_ref, w2_ref, gf_ref, out_ref, *, final_norm):
    x = x_ref[...]
    n = _rms(x, g_ref[...]).astype(BF16)
    acc = x
    for c in range(D_FF // FF_CHUNK):
        cols = slice(c * FF_CHUNK, (c + 1) * FF_CHUNK)
        h = jnp.maximum(_dot(n, w1_ref[:, cols]), 0.0)
        acc = acc + _dot((h * h).astype(BF16), w2_ref[cols, :])
    if final_norm:
        acc = _rms(acc, gf_ref[...])
    out_ref[...] = acc


def _mlp(x2, g, w1, w2, gf, final_norm):
    t = x2.shape[0]
    return pl.pallas_call(
        functools.partial(_mlp_kernel, final_norm=final_norm),
        grid=(t // ROW_TILE,),
        in_specs=[
            pl.BlockSpec((ROW_TILE, D_MODEL), lambda i: (i, 0)),
            _const_spec((1, D_MODEL)),
            _const_spec((D_MODEL, D_FF)),
            _const_spec((D_FF, D_MODEL)),
            _const_spec((1, D_MODEL)),
        ],
        out_specs=pl.BlockSpec((ROW_TILE, D_MODEL), lambda i: (i, 0)),
        out_shape=jax.ShapeDtypeStruct((t, D_MODEL), F32),
        compiler_params=_cparams(("parallel",)),
        name="mlp_final" if final_norm else "mlp",
    )(x2, g.reshape(1, D_MODEL), w1.astype(BF16), w2.astype(BF16), gf.reshape(1, D_MODEL))


def _rope_rows(xt, cos, sin):
    lo, half = D_NOPE_DIM, D_ROPE_DIM // 2
    x1, x2 = xt[lo:lo + half], xt[lo + half:lo + 2 * half]
    return jnp.concatenate(
        [xt[:lo], x1 * cos - x2 * sin, x2 * cos + x1 * sin, xt[lo + 2 * half:]], axis=0)


ONES_ROWS = 16


def _with_ones_rows(vt, head_rows):
    ones = jnp.ones((ONES_ROWS, vt.shape[1]), F32)
    parts = []
    for r0 in range(0, vt.shape[0], head_rows):
        parts += [vt[r0:r0 + head_rows], ones]
    return jnp.concatenate(parts, axis=0).astype(BF16)


def _cd_in_kernel(x_ref, pos_ref, g_ref, freq_ref, wqkv_ref, wcq_ref, wckv_ref, wkr_ref,
                  qng_ref, kvng_ref, wuq_ref, wuk_ref, wuv_ref,
                  qct_ref, kc_ref, vct_ref, qdt_ref, kd_ref, vdt_ref):
    n = _rms(x_ref[0], g_ref[...]).astype(BF16)
    qc = _dot(n, wqkv_ref[:, :C_WIDTH])
    qct_ref[0, 0] = (qc * (C_HEAD_DIM ** -0.5 * LOG2E)).T.astype(BF16)
    vct_ref[0, 0] = _with_ones_rows(_dot(n, wqkv_ref[:, 2 * C_WIDTH:]).T, 2 * C_HEAD_DIM)

    ang = freq_ref[...] * pos_ref[0, 0].astype(F32)
    cos, sin = jnp.cos(ang), jnp.sin(ang)

    cq = _rms(_dot(n, wcq_ref[...]), qng_ref[...]).astype(BF16)
    q = _dot(cq, wuq_ref[...])
    ckv = _rms(_dot(n, wckv_ref[...]), kvng_ref[...]).astype(BF16)
    k_nope = _dot(ckv, wuk_ref[...])
    vdt_ref[0, 0] = _with_ones_rows(_dot(ckv, wuv_ref[...]).T, D_V_DIM)
    k_rope = _rope_rows(_dot(n, wkr_ref[...]).T, cos, sin).T
    scale = (D_NOPE_DIM + D_ROPE_DIM) ** -0.5 * LOG2E
    for h in range(D_HEADS):
        cols = slice(h * D_PAD_DIM, (h + 1) * D_PAD_DIM)
        qdt_ref[0, 0, cols, :] = (_rope_rows(q[:, cols].T, cos, sin) * scale).astype(BF16)
        kd_ref[0, :, cols] = (k_nope[:, cols] + k_rope).astype(BF16)
    kc_ref[0] = _dot(n, wqkv_ref[:, C_WIDTH:2 * C_WIDTH]).astype(BF16)


def _pad_heads(w, heads, width):
    k = w.shape[0]
    w = w.reshape(k, heads, width)
    return jnp.pad(w, ((0, 0), (0, 0), (0, D_PAD_DIM - width))).reshape(k, heads * D_PAD_DIM)


def _cd_in(x3, positions, g, w_in, q_norm_g, kv_norm_g, w_uq, w_uk, w_uv):
    b, s, _ = x3.shape
    o3 = 3 * C_WIDTH
    o4 = o3 + D_Q_RANK
    o5 = o4 + D_KV_RANK
    wqkv = w_in[:, :o3].astype(BF16)
    wcq = w_in[:, o3:o4].astype(BF16)
    wckv = w_in[:, o4:o5].astype(BF16)
    wkr = jnp.pad(w_in[:, o5:], ((0, 0), (D_NOPE_DIM, D_PAD_DIM - D_NOPE_DIM - D_ROPE_DIM))).astype(BF16)
    wuq = _pad_heads(w_uq, D_HEADS, D_NOPE_DIM + D_ROPE_DIM).astype(BF16)
    wuk = _pad_heads(w_uk, D_HEADS, D_NOPE_DIM).astype(BF16)
    half = D_ROPE_DIM // 2
    freq = (ROPE_THETA ** (-jnp.arange(half, dtype=F32) / half)).reshape(half, 1)
    dw = D_HEADS * D_PAD_DIM
    vc_rows = C_HEADS * (2 * C_HEAD_DIM + ONES_ROWS)
    vd_rows = D_HEADS * (D_V_DIM + ONES_ROWS)
    nt = s // ROW_TILE
    row = lambda w: pl.BlockSpec((1, ROW_TILE, w), lambda bi, i: (bi, i, 0))
    colmajor = lambda w: pl.BlockSpec((1, 1, w, ROW_TILE), lambda bi, i: (bi, i, 0, 0))
    return pl.pallas_call(
        _cd_in_kernel,
        grid=(b, s // ROW_TILE),
        in_specs=[
            row(D_MODEL), colmajor(1), _const_spec((1, D_MODEL)), _const_spec((half, 1)),
            _const_spec((D_MODEL, o3)), _const_spec((D_MODEL, D_Q_RANK)),
            _const_spec((D_MODEL, D_KV_RANK)), _const_spec((D_MODEL, D_PAD_DIM)),
            _const_spec((1, D_Q_RANK)), _const_spec((1, D_KV_RANK)),
            _const_spec((D_Q_RANK, dw)), _const_spec((D_KV_RANK, dw)),
            _const_spec((D_KV_RANK, D_HEADS * D_V_DIM)),
        ],
        out_specs=[colmajor(C_WIDTH), row(C_WIDTH), colmajor(vc_rows), colmajor(dw), row(dw),
                   colmajor(vd_rows)],
        out_shape=[
            jax.ShapeDtypeStruct((b, nt, C_WIDTH, ROW_TILE), BF16),
            jax.ShapeDtypeStruct((b, s, C_WIDTH), BF16),
            jax.ShapeDtypeStruct((b, nt, vc_rows, ROW_TILE), BF16),
            jax.ShapeDtypeStruct((b, nt, dw, ROW_TILE), BF16),
            jax.ShapeDtypeStruct((b, s, dw), BF16),
            jax.ShapeDtypeStruct((b, nt, vd_rows, ROW_TILE), BF16),
        ],
        compiler_params=_cparams(("parallel", "parallel")),
        name="cd_in_proj",
    )(x3, positions.reshape(b, nt, 1, ROW_TILE), g.reshape(1, D_MODEL), freq, wqkv, wcq, wckv, wkr,
      q_norm_g.reshape(1, D_Q_RANK), kv_norm_g.reshape(1, D_KV_RANK), wuq, wuk,
      w_uv.astype(BF16))


def _causal_kernel(qt_ref, k_ref, vt_ref, *rest, mode, lam_init, n_tiles):
    if mode == "diff":
        lq1_ref, lk1_ref, lq2_ref, lk2_ref, g_ref, out_ref = rest[:6]
    else:
        out_ref = rest[0]
    s_bufs, mx_bufs = rest[-11:-7], rest[-7:-3]
    m_ref, l_ref, acc_ref = rest[-3:]
    t = ATTN_TILE
    v_dims = 2 * C_HEAD_DIM if mode == "diff" else D_V_DIM
    v_rows = v_dims + ONES_ROWS
    n_pairs = n_tiles * (n_tiles + 1) // 2
    assert n_tiles >= 2 and n_pairs % 2 == 0
    feat = lax.broadcasted_iota(jnp.int32, (128, t), 0)

    def score(pair, buf, j):
        qi, kb = pair
        rows = pl.ds(pl.multiple_of(kb * t, t), t)
        if mode == "diff":
            qt = qt_ref[0, qi]
            mine = (feat < C_HEAD_DIM) if j == 0 else (feat >= C_HEAD_DIM)
            q, k = jnp.where(mine, qt, jnp.zeros_like(qt)), k_ref[0, rows, :]
        else:
            cols = slice(j * D_PAD_DIM, (j + 1) * D_PAD_DIM)
            q, k = qt_ref[0, qi, cols, :], k_ref[0, rows, cols]
        s = _dot(k, q)
        s_bufs[buf][j] = s
        mx_bufs[buf][j] = jnp.max(s, axis=0, keepdims=True)

    def finish_tile(qi):
        a1 = acc_ref[0] / l_ref[0]
        a2 = acc_ref[1] / l_ref[1]
        if mode == "diff":
            lam = (jnp.exp(jnp.sum(lq1_ref[...] * lk1_ref[...], axis=-1, keepdims=True))
                   - jnp.exp(jnp.sum(lq2_ref[...] * lk2_ref[...], axis=-1, keepdims=True))
                   + lam_init)
            d = a1 - lam * a2
            out = (d * lax.rsqrt(jnp.mean(d * d, axis=0, keepdims=True) + EPS) * g_ref[...]
                   * (1.0 - lam_init))
        else:
            out = jnp.concatenate([a1, a2], axis=0)
        out_ref[0, pl.ds(pl.multiple_of(qi * t, t), t), :] = out.T.astype(BF16)

    def consume(pair, buf, j, diagonal):
        qi, kb = pair
        first = kb == 0
        s = s_bufs[buf][j]
        if diagonal:
            kr = lax.broadcasted_iota(jnp.int32, (t, t), 0)
            qc = lax.broadcasted_iota(jnp.int32, (t, t), 1)
            s = jnp.where(kr <= qc, s, NEG)
            block_max = jnp.max(s, axis=0, keepdims=True)
        else:
            block_max = mx_bufs[buf][j]
        m = jnp.where(first, NEG, m_ref[j])
        l = jnp.where(first, 0.0, l_ref[j])
        m_new = jnp.maximum(m, block_max)
        alpha = jnp.exp2(m - m_new)
        p = jnp.exp2((s - m_new).astype(BF16))
        m_ref[j] = m_new
        if mode == "diff":
            vt = vt_ref[0, kb]
        else:
            vt = vt_ref[0, kb, j * v_rows:(j + 1) * v_rows, :]
        pv = _dot(vt, p)
        l_ref[j] = alpha * l + pv[v_dims:v_dims + 1]
        acc_ref[j] = alpha * acc_ref[j] + pv[:v_dims]
        if diagonal and j == 1:
            finish_tile(qi)

    def advance(pair):
        qi, kb = pair
        end = kb == qi
        nqi, nkb = jnp.where(end, qi + 1, qi), jnp.where(end, 0, kb + 1)
        done = nqi >= n_tiles
        return jnp.where(done, qi, nqi), jnp.where(done, kb, nkb)

    acc_ref[...] = jnp.zeros(acc_ref.shape, F32)
    pair0 = (jnp.int32(0), jnp.int32(0))
    pair1 = advance(pair0)
    for j in range(2):
        score(pair0, 0, j)
        score(pair1, 1, j)

    def half_trip(pa, pb, half):
        pc = advance(pb)
        pd = advance(pc)
        diag_a, diag_b = pa[0] == pa[1], pb[0] == pb[1]
        ra, rb, wc, wd = 2 * half, 2 * half + 1, 2 - 2 * half, 3 - 2 * half

        def run(da, db):
            score(pc, wc, 0)
            score(pc, wc, 1)
            consume(pa, ra, 0, da)
            score(pd, wd, 0)
            consume(pa, ra, 1, da)
            consume(pb, rb, 0, db)
            score(pd, wd, 1)
            consume(pb, rb, 1, db)

        @pl.when(diag_a | diag_b)
        def _():
            pl.when(diag_a)(lambda: run(True, False))
            pl.when(diag_b)(lambda: run(False, True))

        pl.when(jnp.logical_not(diag_a | diag_b))(lambda: run(False, False))
        return pc, pd

    def body(_, carry):
        pa, pb = carry[:2], carry[2:]
        for half in range(2):
            pa, pb = half_trip(pa, pb, half)
        return (*pa, *pb)

    assert n_pairs % 4 == 0
    lax.fori_loop(0, n_pairs // 4, body, (*pair0, *pair1))


def _causal_attention(qt, k, vt, mode, extras=(), lam_init=0.0):
    b, s, _ = k.shape
    t = ATTN_TILE
    assert qt.shape[3] == t and vt.shape[3] == t
    qw = 128 if mode == "diff" else 2 * D_PAD_DIM
    groups = k.shape[2] // qw
    in_specs = [
        pl.BlockSpec((1, s // t, qw, t), lambda bi, h: (bi, 0, h, 0)),
        pl.BlockSpec((1, s, qw), lambda bi, h: (bi, 0, h)),
        pl.BlockSpec((1, s // t, vt.shape[2] // groups, t), lambda bi, h: (bi, 0, h, 0)),
    ] + [_const_spec(e.shape) for e in extras]
    stats = [pltpu.VMEM((2, 1, t), F32)] * 2
    return pl.pallas_call(
        functools.partial(_causal_kernel, mode=mode, lam_init=lam_init, n_tiles=s // t),
        grid=(b, groups),
        in_specs=in_specs,
        out_specs=pl.BlockSpec((1, s, 128), lambda bi, h: (bi, 0, h)),
        out_shape=jax.ShapeDtypeStruct((b, s, groups * 128), BF16),
        scratch_shapes=[
            *[pltpu.VMEM((2, t, t), F32)] * 4,
            *stats, *stats,
            *stats,
            pltpu.VMEM((2, 128 if mode == "diff" else D_V_DIM, t), F32),
        ],
        compiler_params=_cparams(("parallel", "parallel")),
        name=f"causal_attn_{mode}",
    )(qt, k, vt, *extras)


def _cd_out_kernel(x_ref, yc_ref, yd_ref, wc_ref, wd_ref, out_ref):
    out_ref[...] = x_ref[...] + _dot(yc_ref[...], wc_ref[...]) + _dot(yd_ref[...], wd_ref[...])


def _cd_out(x2, yc, yd, w_out):
    t = x2.shape[0]
    row = lambda w: pl.BlockSpec((WIDE_ROW_TILE, w), lambda i: (i, 0))
    dvw = D_HEADS * D_V_DIM
    return pl.pallas_call(
        _cd_out_kernel,
        grid=(t // WIDE_ROW_TILE,),
        in_specs=[row(D_MODEL), row(C_WIDTH), row(dvw),
                  _const_spec((C_WIDTH, D_MODEL)), _const_spec((dvw, D_MODEL))],
        out_specs=row(D_MODEL),
        out_shape=jax.ShapeDtypeStruct((t, D_MODEL), F32),
        compiler_params=_cparams(("parallel",)),
        name="cd_out_proj",
    )(x2, yc, yd, w_out[:C_WIDTH].astype(BF16), w_out[C_WIDTH:].astype(BF16))


def kernel(x, mem, positions, norm_mix_g, norm_cross_g, norm_mem_g, cross_wq, cross_wkv, cross_wo,
           norm_mlp_g, mlp_w1, mlp_w2, ab_w_in, ab_w_out, ab_conv_w, ab_conv_b, ab_ln_g, ab_ln_b,
           cd_w_in, cd_w_out, diff_lq1, diff_lk1, diff_lq2, diff_lk2, diff_subln_g, mla_q_norm_g,
           mla_kv_norm_g, mla_w_uq, mla_w_uk, mla_w_uv, final_norm_g):
    b, s, d = x.shape
    t = b * s
    mem_len = mem.shape[1]
    depth = norm_mix_g.shape[0]
    xw = X_HEADS * X_HEAD_DIM
    x2 = x.reshape(t, d)
    for i in range(depth):
        j = i // 2
        if i % 2 == 0:
            *views, glu = _ab_in(x2.reshape(b, s, d), norm_mix_g[i], ab_w_in[j])
            outs, lses = zip(*[_dilated_pattern(view, dil)
                               for view, (_, dil) in zip(views, A_PATTERNS)])
            yb = _conv_module(glu, ab_conv_w[j], ab_conv_b[j], ab_ln_g[j], ab_ln_b[j])
            x2 = _ab_out(x2.reshape(b, s, d), outs, lses, yb, ab_w_out[j]).reshape(t, d)
        else:
            qc, kc, vc, qd, kd, vd = _cd_in(
                x2.reshape(b, s, d), positions, norm_mix_g[i], cd_w_in[j], mla_q_norm_g[j],
                mla_kv_norm_g[j], mla_w_uq[j], mla_w_uk[j], mla_w_uv[j])
            lam_init = 0.8 - 0.6 * math.exp(-0.3 * i)
            vec = lambda a: a.reshape(1, -1)
            yc = _causal_attention(
                qc, kc, vc, "diff",
                extras=(vec(diff_lq1[j]), vec(diff_lk1[j]), vec(diff_lq2[j]), vec(diff_lk2[j]),
                        diff_subln_g[j].reshape(-1, 1)),
                lam_init=lam_init)
            yd = _causal_attention(qd, kd, vd, "mla")
            x2 = _cd_out(x2, yc.reshape(t, C_WIDTH), yd.reshape(t, D_HEADS * D_V_DIM), cd_w_out[j])
        km, vm = _mem_kv(mem.reshape(b * mem_len, d), norm_mem_g[i], cross_wkv[i])
        x2 = _cross_attention(x2.reshape(b, s, d), norm_cross_g[i], cross_wq[i],
                              km.reshape(b, mem_len, xw), vm.reshape(b, mem_len, xw),
                              cross_wo[i]).reshape(t, d)
        x2 = _mlp(x2, norm_mlp_g[i], mlp_w1[i], mlp_w2[i], final_norm_g, i == depth - 1)
    return x2.reshape(b, s, d)
```

```python
import functools
import math

import jax
import jax.numpy as jnp
from jax import lax
from jax.experimental import pallas as pl
from jax.experimental.pallas import tpu as pltpu

F32 = jnp.float32
BF16 = jnp.bfloat16

D_MODEL = 1024
EPS = 1e-6
LOG2E = math.log2(math.e)
NEG = -1e30

A_HEADS = 8
A_HEAD_DIM = 64
A_WIDTH = A_HEADS * A_HEAD_DIM
A_PATTERNS = ((128, 1), (512, 4), (2048, 16))
A_BLK = 128
assert all(window // dilation == A_BLK for window, dilation in A_PATTERNS)
LSE_LANES = 128
assert A_HEADS <= LSE_LANES
B_CHANNELS = 512
B_CONV_WIDTH = 31
B_HALO = 32
C_HEADS = 4
C_HEAD_DIM = 64
C_WIDTH = C_HEADS * 2 * C_HEAD_DIM
D_HEADS = 8
D_NOPE_DIM = 64
D_ROPE_DIM = 32
D_V_DIM = 64
D_Q_RANK = 384
D_KV_RANK = 256
D_PAD_DIM = 128
ROPE_THETA = 10000.0
X_HEADS = 4
X_HEAD_DIM = 128
D_FF = 4 * D_MODEL
FF_CHUNK = 1024

ROW_TILE = 512
WIDE_ROW_TILE = 1024
ATTN_TILE = ROW_TILE
VMEM_LIMIT = 56 * 1024 * 1024


def _cparams(sem):
    return pltpu.CompilerParams(dimension_semantics=sem, vmem_limit_bytes=VMEM_LIMIT)


def _rms(x, g):
    return x * lax.rsqrt(jnp.mean(x * x, axis=-1, keepdims=True) + EPS) * g


def _dot(a, b):
    return jnp.dot(a, b, preferred_element_type=F32)


def _dot_t(a, b):
    return lax.dot_general(a, b, (((1,), (1,)), ((), ())), preferred_element_type=F32)


def _const_spec(shape):
    nd = len(shape)
    return pl.BlockSpec(shape, lambda *_: (0,) * nd)


def _regather_dilations():
    dils = sorted(d for _, d in A_PATTERNS if d > 1)
    return [e for e in dils if any(d > e and d % e == 0 for d in dils)]


def _ab_in_kernel(x_ref, g_ref, wqkv_ref, wu_ref, wg_ref, *rest):
    n_keep = len(_regather_dilations())
    views, glu_ref = rest[:-2 - n_keep], rest[-2 - n_keep]
    qkv_scr, kept = rest[-1 - n_keep], dict(zip(_regather_dilations(), rest[len(rest) - n_keep:]))
    n = _rms(x_ref[0], g_ref[...]).astype(BF16)
    width = 3 * A_WIDTH
    dense = [v for (_, d), v in zip(A_PATTERNS, views) if d == 1]
    chunk = 256
    assert A_WIDTH % chunk == 0
    for c0 in range(0, width, chunk):
        part = _dot(n, wqkv_ref[:, c0:c0 + chunk])
        if c0 < A_WIDTH:
            part = part * (A_HEAD_DIM ** -0.5 * LOG2E)
        for c in range(c0 // 128, (c0 + chunk) // 128):
            qkv_scr[c] = part[:, c * 128 - c0:(c + 1) * 128 - c0]
        for view_ref in dense:
            view_ref[0, :, c0:c0 + chunk] = part.astype(BF16)
    sources = {1: qkv_scr}
    for d, view_ref in sorted((d, v) for (_, d), v in zip(A_PATTERNS, views) if d > 1):
        e = max(e for e in sources if d % e == 0)
        step, n_d, n_e = d // e, ROW_TILE // d, ROW_TILE // e
        for r in range(d):
            start = (r % e) * n_e + r // e
            for c in range(width // 128):
                rows = sources[e][c, pl.ds(start, n_d, stride=step), :]
                view_ref[0, :, r * width + c * 128:r * width + (c + 1) * 128] = rows.astype(BF16)
                if d in kept:
                    kept[d][c, r * n_d:(r + 1) * n_d, :] = rows
        if d in kept:
            sources[d] = kept[d]
    u = _dot(n, wu_ref[...])
    gate = _dot(n, wg_ref[...])
    glu_ref[0] = u * jax.nn.sigmoid(gate)


def _ab_in(x3, g, w_in):
    b, s, _ = x3.shape
    width = 3 * A_WIDTH
    wqkv = w_in[:, :width].astype(BF16)
    wu = w_in[:, width:width + B_CHANNELS].astype(BF16)
    wg = w_in[:, width + B_CHANNELS:].astype(BF16)
    row = lambda rows, w: pl.BlockSpec((1, rows, w), lambda bi, i: (bi, i, 0))
    dils = [d for _, d in A_PATTERNS]
    return pl.pallas_call(
        _ab_in_kernel,
        grid=(b, s // ROW_TILE),
        in_specs=[
            row(ROW_TILE, D_MODEL),
            _const_spec((1, D_MODEL)),
            _const_spec((D_MODEL, width)),
            _const_spec((D_MODEL, B_CHANNELS)),
            _const_spec((D_MODEL, B_CHANNELS)),
        ],
        out_specs=[row(ROW_TILE // d, d * width) for d in dils] + [row(ROW_TILE, B_CHANNELS)],
        out_shape=[jax.ShapeDtypeStruct((b, s // d, d * width), BF16) for d in dils]
        + [jax.ShapeDtypeStruct((b, s, B_CHANNELS), F32)],
        scratch_shapes=[pltpu.VMEM((width // 128, ROW_TILE, 128), F32)]
        * (1 + len(_regather_dilations())),
        compiler_params=_cparams(("parallel", "parallel")),
        name="ab_in_proj",
    )(x3, g.reshape(1, D_MODEL), wqkv, wu, wg)


def _dilated_kernel(q_ref, kp_ref, kc_ref, vp_ref, vc_ref, o_ref, lse_ref, s_ref, *, rows):
    first_tile = pl.program_id(2) == 0
    qi = lax.broadcasted_iota(jnp.int32, (A_BLK, 2 * A_BLK), 0)
    kj = lax.broadcasted_iota(jnp.int32, (A_BLK, 2 * A_BLK), 1)
    band = ((kj < A_BLK) & (kj >= qi)) | ((kj >= A_BLK) & (kj - A_BLK <= qi))
    lane = lax.broadcasted_iota(jnp.int32, (A_BLK, 128), 1)
    low = lane < A_HEAD_DIM
    pairs = A_WIDTH // 128
    nblk = rows // A_BLK

    def blk(i):
        return slice(i * A_BLK, (i + 1) * A_BLK)

    def prev_and_cur(i, prev_ref, cur_ref, cols):
        prev = prev_ref[0, :, cols] if i == 0 else cur_ref[0, blk(i - 1), cols]
        return jnp.concatenate([prev, cur_ref[0, blk(i), cols]], axis=0)

    def scores(i):
        for pr in range(pairs):
            cols = slice(pr * 128, (pr + 1) * 128)
            q = q_ref[0, blk(i), cols]
            k = prev_and_cur(i, kp_ref, kc_ref, cols)
            zero = jnp.zeros_like(q)
            both = jnp.concatenate([jnp.where(low, q, zero), jnp.where(low, zero, q)], axis=0)
            s = _dot_t(both, k)
            s_ref[i % 2, 2 * pr] = s[:A_BLK]
            s_ref[i % 2, 2 * pr + 1] = s[A_BLK:]

    def consume(i):
        valid = band & ((kj >= A_BLK) | jnp.logical_not(first_tile)) if i == 0 else band
        lse_tile = jnp.zeros((A_BLK, 128), F32)
        for pr in range(pairs):
            cols = slice(pr * 128, (pr + 1) * 128)
            v = prev_and_cur(i, vp_ref, vc_ref, cols)
            ps, ls = [], []
            for half in range(2):
                s = jnp.where(valid, s_ref[i % 2, 2 * pr + half], NEG)
                m = jnp.max(s, axis=-1, keepdims=True)
                p = jnp.exp2(s - m)
                l = jnp.sum(p, axis=-1, keepdims=True)
                ps.append(p.astype(BF16))
                ls.append(l)
                lse_tile = jnp.where(lane == 2 * pr + half, m + jnp.log2(l), lse_tile)
            pv = _dot(jnp.concatenate(ps, axis=0), v)
            out = jnp.where(low, pv[:A_BLK] / ls[0], pv[A_BLK:] / ls[1])
            o_ref[0, blk(i), cols] = out.astype(BF16)
        lse_ref[0, blk(i), :] = lse_tile

    scores(0)
    for i in range(nblk):
        if i + 1 < nblk:
            scores(i + 1)
        consume(i)


def _dilated_pattern(view, dilation):
    b, sub, _ = view.shape
    rows = min(sub, WIDE_ROW_TILE)
    assert sub % rows == 0 and rows % A_BLK == 0
    blocks_per_tile = rows // A_BLK
    prev_row = lambda n: jnp.maximum(n * blocks_per_tile - 1, 0)
    cur_spec = lambda part: pl.BlockSpec((1, rows, A_WIDTH),
                                         lambda bi, r, n: (bi, n, 3 * r + part))
    prev_spec = lambda part: pl.BlockSpec((1, A_BLK, A_WIDTH),
                                          lambda bi, r, n: (bi, prev_row(n), 3 * r + part))
    out_spec = lambda w: pl.BlockSpec((1, rows, w), lambda bi, r, n: (bi, n, r))
    return pl.pallas_call(
        functools.partial(_dilated_kernel, rows=rows),
        grid=(b, dilation, sub // rows),
        in_specs=[cur_spec(0), prev_spec(1), cur_spec(1), prev_spec(2), cur_spec(2)],
        out_specs=[out_spec(A_WIDTH), out_spec(LSE_LANES)],
        out_shape=[jax.ShapeDtypeStruct((b, sub, dilation * A_WIDTH), BF16),
                   jax.ShapeDtypeStruct((b, sub, dilation * LSE_LANES), F32)],
        scratch_shapes=[pltpu.VMEM((2, A_HEADS, A_BLK, 2 * A_BLK), F32)],
        compiler_params=_cparams(("parallel", "parallel", "arbitrary")),
        name=f"dilated_attn_d{dilation}",
    )(view, view, view, view, view)


CONV_ROWS = 128
SUBLANES = 8


def _conv_kernel(halo_ref, cur_ref, w_ref, cb_ref, g_ref, b_ref, out_ref, buf_ref, *, rows):
    first_tile = pl.program_id(1) == 0
    halo = halo_ref[0]
    buf_ref[0:B_HALO, :] = jnp.where(first_tile, jnp.zeros_like(halo), halo)
    buf_ref[B_HALO:B_HALO + rows, :] = cur_ref[0]
    lead = B_HALO - (B_CONV_WIDTH - 1)

    def chunk(c, carry):
        base = pl.multiple_of(c * CONV_ROWS, CONV_ROWS)
        span = CONV_ROWS + B_HALO
        accs = []
        for c in range(B_CHANNELS // 128):
            cols = slice(c * 128, (c + 1) * 128)
            win = buf_ref[pl.ds(base, span), cols]
            acc = jnp.zeros((CONV_ROWS, 128), F32)
            for r in range(SUBLANES):
                shifted = pltpu.roll(win, span - r, 0) if r else win
                for tap in range(B_CONV_WIDTH):
                    if (lead + tap) % SUBLANES == r:
                        a = (lead + tap) // SUBLANES * SUBLANES
                        acc = acc + shifted[a:a + CONV_ROWS, :] * w_ref[tap:tap + 1, cols]
            accs.append(acc)
        y = jnp.concatenate(accs, axis=-1) + cb_ref[...]
        mu = jnp.mean(y, axis=-1, keepdims=True)
        yc = y - mu
        var = jnp.mean(yc * yc, axis=-1, keepdims=True)
        z = yc * lax.rsqrt(var + EPS) * g_ref[...] + b_ref[...]
        out_ref[0, pl.ds(base, CONV_ROWS), :] = (z * jax.nn.sigmoid(z)).astype(BF16)
        return carry

    lax.fori_loop(0, rows // CONV_ROWS, chunk, 0)


def _conv_module(glu, conv_w, conv_b, ln_g, ln_b):
    b, s, c = glu.shape
    rows = WIDE_ROW_TILE
    halo_blocks = rows // B_HALO
    vec = lambda a: a.reshape(1, c)
    return pl.pallas_call(
        functools.partial(_conv_kernel, rows=rows),
        grid=(b, s // rows),
        in_specs=[
            pl.BlockSpec((1, B_HALO, c), lambda bi, n: (bi, jnp.maximum(n * halo_blocks - 1, 0), 0)),
            pl.BlockSpec((1, rows, c), lambda bi, n: (bi, n, 0)),
            _const_spec((B_CONV_WIDTH, c)),
            _const_spec((1, c)), _const_spec((1, c)), _const_spec((1, c)),
        ],
        out_specs=pl.BlockSpec((1, rows, c), lambda bi, n: (bi, n, 0)),
        out_shape=jax.ShapeDtypeStruct((b, s, c), BF16),
        scratch_shapes=[pltpu.VMEM((B_HALO + rows, c), F32)],
        compiler_params=_cparams(("parallel", "arbitrary")),
        name="conformer_conv",
    )(glu, glu, conv_w.reshape(B_CONV_WIDTH, c), vec(conv_b), vec(ln_g), vec(ln_b))


def _finer_dilation(d):
    return max([e for _, e in A_PATTERNS if 1 < e < d and d % e == 0], default=1)


def _reinterleave(src_ref, d, dst_scr, mid_scr, convert):
    tiles = dst_scr.shape[0]
    width = tiles * 128
    e = _finer_dilation(d)
    first = mid_scr if e > 1 else dst_scr
    step, n_d, n_e = d // e, ROW_TILE // d, ROW_TILE // e
    for r in range(d):
        dst = pl.ds((r % e) * n_e + r // e, n_d, stride=step)
        for c in range(tiles):
            first[c, dst, :] = convert(src_ref[0, :, r * width + c * 128:r * width + (c + 1) * 128])
    if e > 1:
        for q in range(e):
            for c in range(tiles):
                dst_scr[c, pl.ds(q, n_e, stride=e), :] = mid_scr[c, q * n_e:(q + 1) * n_e, :]
    return jnp.concatenate([dst_scr[c] for c in range(tiles)], axis=-1)


def _ab_out_kernel(x_ref, *rest):
    n_pat = len(A_PATTERNS)
    o_refs, l_refs = rest[:n_pat], rest[n_pat:2 * n_pat]
    yb_ref, expand_ref, wa_ref, wb_ref, out_ref = rest[2 * n_pat:2 * n_pat + 5]
    scr = list(rest[2 * n_pat + 5:])
    outs, lses = [], []
    for idx, (_, d) in enumerate(A_PATTERNS):
        if d == 1:
            outs.append(o_refs[idx][0].astype(F32))
            lses.append(l_refs[idx][0])
            continue
        o_scr, l_scr = scr.pop(0), scr.pop(0)
        o_mid, l_mid = (scr.pop(0), scr.pop(0)) if _finer_dilation(d) > 1 else (None, None)
        outs.append(_reinterleave(o_refs[idx], d, o_scr, o_mid, lambda v: v.astype(F32)))
        lses.append(_reinterleave(l_refs[idx], d, l_scr, l_mid, lambda v: v))
    m = functools.reduce(jnp.maximum, lses)
    es = [jnp.exp2(l - m) for l in lses]
    inv = 1.0 / sum(es)
    ya = outs[-1]
    for e, o in zip(es[:-1], outs[:-1]):
        w = e * inv
        hi = w.astype(BF16)
        lo = (w - hi.astype(F32)).astype(BF16)
        ya = ya + (_dot(hi, expand_ref[...]) + _dot(lo, expand_ref[...])) * (o - outs[-1])
    out_ref[0] = (x_ref[0] + _dot(ya.astype(BF16), wa_ref[...])
                  + _dot(yb_ref[0], wb_ref[...]))


def _ab_out(x3, outs, lses, yb, w_out):
    b, s, _ = x3.shape
    row = lambda rows, w: pl.BlockSpec((1, rows, w), lambda bi, i: (bi, i, 0))
    o_views = [row(ROW_TILE // d, d * A_WIDTH) for _, d in A_PATTERNS]
    l_views = [row(ROW_TILE // d, d * LSE_LANES) for _, d in A_PATTERNS]
    scratch = []
    for _, d in A_PATTERNS:
        if d > 1:
            scratch += [pltpu.VMEM((A_WIDTH // 128, ROW_TILE, 128), F32),
                        pltpu.VMEM((LSE_LANES // 128, ROW_TILE, 128), F32)] * (
                2 if _finer_dilation(d) > 1 else 1)
    expand = jnp.arange(LSE_LANES)[:, None] == jnp.arange(A_WIDTH)[None, :] // A_HEAD_DIM
    return pl.pallas_call(
        _ab_out_kernel,
        grid=(b, s // ROW_TILE),
        in_specs=[row(ROW_TILE, D_MODEL)] + o_views + l_views + [
            row(ROW_TILE, B_CHANNELS), _const_spec((LSE_LANES, A_WIDTH)),
            _const_spec((A_WIDTH, D_MODEL)), _const_spec((B_CHANNELS, D_MODEL))],
        out_specs=row(ROW_TILE, D_MODEL),
        out_shape=jax.ShapeDtypeStruct((b, s, D_MODEL), F32),
        scratch_shapes=scratch,
        compiler_params=_cparams(("parallel", "parallel")),
        name="ab_out_proj",
    )(x3, *outs, *lses, yb, expand.astype(BF16),
      w_out[:A_WIDTH].astype(BF16), w_out[A_WIDTH:].astype(BF16))


def _mem_kv_kernel(mem_ref, g_ref, wk_ref, wv_ref, k_ref, v_ref):
    n = _rms(mem_ref[...], g_ref[...]).astype(BF16)
    k_ref[...] = _dot(n, wk_ref[...]).astype(BF16)
    v_ref[...] = _dot(n, wv_ref[...]).astype(BF16)


def _mem_kv(mem2, g, wkv):
    rows = mem2.shape[0]
    w = X_HEADS * X_HEAD_DIM
    return pl.pallas_call(
        _mem_kv_kernel,
        grid=(1,),
        in_specs=[_const_spec((rows, D_MODEL)), _const_spec((1, D_MODEL)),
                  _const_spec((D_MODEL, w)), _const_spec((D_MODEL, w))],
        out_specs=[_const_spec((rows, w)), _const_spec((rows, w))],
        out_shape=[jax.ShapeDtypeStruct((rows, w), BF16)] * 2,
        compiler_params=_cparams(("arbitrary",)),
        name="mem_kv_proj",
    )(mem2, g.reshape(1, D_MODEL), wkv[:, :w].astype(BF16), wkv[:, w:].astype(BF16))


def _cross_kernel(x_ref, g_ref, wq_ref, k_ref, v_ref, wo_ref, out_ref):
    x = x_ref[0]
    n = _rms(x, g_ref[...]).astype(BF16)
    q = (_dot(n, wq_ref[...]) * (X_HEAD_DIM ** -0.5)).astype(BF16)
    heads = []
    for h in range(X_HEADS):
        cols = slice(h * X_HEAD_DIM, (h + 1) * X_HEAD_DIM)
        s = _dot_t(q[:, cols], k_ref[0, :, cols])
        m = jnp.max(s, axis=-1, keepdims=True)
        p = jnp.exp(s - m)
        l = jnp.sum(p, axis=-1, keepdims=True)
        heads.append((_dot(p.astype(BF16), v_ref[0, :, cols]) / l).astype(BF16))
    o = jnp.concatenate(heads, axis=-1)
    out_ref[0] = x + _dot(o, wo_ref[...])


def _cross_attention(x3, g, wq, k, v, wo):
    b, s, _ = x3.shape
    m = k.shape[1]
    w = X_HEADS * X_HEAD_DIM
    rows = WIDE_ROW_TILE
    return pl.pallas_call(
        _cross_kernel,
        grid=(b, s // rows),
        in_specs=[
            pl.BlockSpec((1, rows, D_MODEL), lambda bi, i: (bi, i, 0)),
            _const_spec((1, D_MODEL)),
            _const_spec((D_MODEL, w)),
            pl.BlockSpec((1, m, w), lambda bi, i: (bi, 0, 0)),
            pl.BlockSpec((1, m, w), lambda bi, i: (bi, 0, 0)),
            _const_spec((w, D_MODEL)),
        ],
        out_specs=pl.BlockSpec((1, rows, D_MODEL), lambda bi, i: (bi, i, 0)),
        out_shape=jax.ShapeDtypeStruct((b, s, D_MODEL), F32),
        compiler_params=_cparams(("parallel", "parallel")),
        name="cross_attn",
    )(x3, g.reshape(1, D_MODEL), wq.astype(BF16), k, v, wo.astype(BF16))


def _mlp_kernel(x_ref, g_ref, w1_ref, w2_ref, gf_ref, out_ref, *, final_norm):
    x = x_ref[...]
    n = _rms(x, g_ref[...]).astype(BF16)
    acc = x
    for c in range(D_FF // FF_CHUNK):
        cols = slice(c * FF_CHUNK, (c + 1) * FF_CHUNK)
        h = jnp.maximum(_dot(n, w1_ref[:, cols]), 0.0)
        acc = acc + _dot((h * h).astype(BF16), w2_ref[cols, :])
    if final_norm:
        acc = _rms(acc, gf_ref[...])
    out_ref[...] = acc


def _mlp(x2, g, w1, w2, gf, final_norm):
    t = x2.shape[0]
    return pl.pallas_call(
        functools.partial(_mlp_kernel, final_norm=final_norm),
        grid=(t // ROW_TILE,),
        in_specs=[
            pl.BlockSpec((ROW_TILE, D_MODEL), lambda i: (i, 0)),
            _const_spec((1, D_MODEL)),
            _const_spec((D_MODEL, D_FF)),
            _const_spec((D_FF, D_MODEL)),
            _const_spec((1, D_MODEL)),
        ],
        out_specs=pl.BlockSpec((ROW_TILE, D_MODEL), lambda i: (i, 0)),
        out_shape=jax.ShapeDtypeStruct((t, D_MODEL), F32),
        compiler_params=_cparams(("parallel",)),
        name="mlp_final" if final_norm else "mlp",
    )(x2, g.reshape(1, D_MODEL), w1.astype(BF16), w2.astype(BF16), gf.reshape(1, D_MODEL))


def _rope_rows(xt, cos, sin):
    lo, half = D_NOPE_DIM, D_ROPE_DIM // 2
    x1, x2 = xt[lo:lo + half], xt[lo + half:lo + 2 * half]
    return jnp.concatenate(
        [xt[:lo], x1 * cos - x2 * sin, x2 * cos + x1 * sin, xt[lo + 2 * half:]], axis=0)


ONES_ROWS = 16


def _with_ones_rows(vt, head_rows):
    ones = jnp.ones((ONES_ROWS, vt.shape[1]), F32)
    parts = []
    for r0 in range(0, vt.shape[0], head_rows):
        parts += [vt[r0:r0 + head_rows], ones]
    return jnp.concatenate(parts, axis=0).astype(BF16)


def _cd_in_kernel(x_ref, pos_ref, g_ref, freq_ref, wqkv_ref, wcq_ref, wckv_ref, wkr_ref,
                  qng_ref, kvng_ref, wuq_ref, wuk_ref, wuv_ref,
                  qct_ref, kc_ref, vct_ref, qdt_ref, kd_ref, vdt_ref):
    n = _rms(x_ref[0], g_ref[...]).astype(BF16)
    qc = _dot(n, wqkv_ref[:, :C_WIDTH])
    qct_ref[0, 0] = (qc * (C_HEAD_DIM ** -0.5 * LOG2E)).T.astype(BF16)
    vct_ref[0, 0] = _with_ones_rows(_dot(n, wqkv_ref[:, 2 * C_WIDTH:]).T, 2 * C_HEAD_DIM)

    ang = freq_ref[...] * pos_ref[0, 0].astype(F32)
    cos, sin = jnp.cos(ang), jnp.sin(ang)

    cq = _rms(_dot(n, wcq_ref[...]), qng_ref[...]).astype(BF16)
    q = _dot(cq, wuq_ref[...])
    ckv = _rms(_dot(n, wckv_ref[...]), kvng_ref[...]).astype(BF16)
    k_nope = _dot(ckv, wuk_ref[...])
    vdt_ref[0, 0] = _with_ones_rows(_dot(ckv, wuv_ref[...]).T, D_V_DIM)
    k_rope = _rope_rows(_dot(n, wkr_ref[...]).T, cos, sin).T
    scale = (D_NOPE_DIM + D_ROPE_DIM) ** -0.5 * LOG2E
    for h in range(D_HEADS):
        cols = slice(h * D_PAD_DIM, (h + 1) * D_PAD_DIM)
        qdt_ref[0, 0, cols, :] = (_rope_rows(q[:, cols].T, cos, sin) * scale).astype(BF16)
        kd_ref[0, :, cols] = (k_nope[:, cols] + k_rope).astype(BF16)
    kc_ref[0] = _dot(n, wqkv_ref[:, C_WIDTH:2 * C_WIDTH]).astype(BF16)


def _pad_heads(w, heads, width):
    k = w.shape[0]
    w = w.reshape(k, heads, width)
    return jnp.pad(w, ((0, 0), (0, 0), (0, D_PAD_DIM - width))).reshape(k, heads * D_PAD_DIM)


def _cd_in(x3, positions, g, w_in, q_norm_g, kv_norm_g, w_uq, w_uk, w_uv):
    b, s, _ = x3.shape
    o3 = 3 * C_WIDTH
    o4 = o3 + D_Q_RANK
    o5 = o4 + D_KV_RANK
    wqkv = w_in[:, :o3].astype(BF16)
    wcq = w_in[:, o3:o4].astype(BF16)
    wckv = w_in[:, o4:o5].astype(BF16)
    wkr = jnp.pad(w_in[:, o5:], ((0, 0), (D_NOPE_DIM, D_PAD_DIM - D_NOPE_DIM - D_ROPE_DIM))).astype(BF16)
    wuq = _pad_heads(w_uq, D_HEADS, D_NOPE_DIM + D_ROPE_DIM).astype(BF16)
    wuk = _pad_heads(w_uk, D_HEADS, D_NOPE_DIM).astype(BF16)
    half = D_ROPE_DIM // 2
    freq = (ROPE_THETA ** (-jnp.arange(half, dtype=F32) / half)).reshape(half, 1)
    dw = D_HEADS * D_PAD_DIM
    vc_rows = C_HEADS * (2 * C_HEAD_DIM + ONES_ROWS)
    vd_rows = D_HEADS * (D_V_DIM + ONES_ROWS)
    nt = s // ROW_TILE
    row = lambda w: pl.BlockSpec((1, ROW_TILE, w), lambda bi, i: (bi, i, 0))
    colmajor = lambda w: pl.BlockSpec((1, 1, w, ROW_TILE), lambda bi, i: (bi, i, 0, 0))
    return pl.pallas_call(
        _cd_in_kernel,
        grid=(b, s // ROW_TILE),
        in_specs=[
            row(D_MODEL), colmajor(1), _const_spec((1, D_MODEL)), _const_spec((half, 1)),
            _const_spec((D_MODEL, o3)), _const_spec((D_MODEL, D_Q_RANK)),
            _const_spec((D_MODEL, D_KV_RANK)), _const_spec((D_MODEL, D_PAD_DIM)),
            _const_spec((1, D_Q_RANK)), _const_spec((1, D_KV_RANK)),
            _const_spec((D_Q_RANK, dw)), _const_spec((D_KV_RANK, dw)),
            _const_spec((D_KV_RANK, D_HEADS * D_V_DIM)),
        ],
        out_specs=[colmajor(C_WIDTH), row(C_WIDTH), colmajor(vc_rows), colmajor(dw), row(dw),
                   colmajor(vd_rows)],
        out_shape=[
            jax.ShapeDtypeStruct((b, nt, C_WIDTH, ROW_TILE), BF16),
            jax.ShapeDtypeStruct((b, s, C_WIDTH), BF16),
            jax.ShapeDtypeStruct((b, nt, vc_rows, ROW_TILE), BF16),
            jax.ShapeDtypeStruct((b, nt, dw, ROW_TILE), BF16),
            jax.ShapeDtypeStruct((b, s, dw), BF16),
            jax.ShapeDtypeStruct((b, nt, vd_rows, ROW_TILE), BF16),
        ],
        compiler_params=_cparams(("parallel", "parallel")),
        name="cd_in_proj",
    )(x3, positions.reshape(b, nt, 1, ROW_TILE), g.reshape(1, D_MODEL), freq, wqkv, wcq, wckv, wkr,
      q_norm_g.reshape(1, D_Q_RANK), kv_norm_g.reshape(1, D_KV_RANK), wuq, wuk,
      w_uv.astype(BF16))


def _causal_kernel(qt_ref, k_ref, vt_ref, *rest, mode, lam_init, n_tiles):
    if mode == "diff":
        lq1_ref, lk1_ref, lq2_ref, lk2_ref, g_ref, out_ref = rest[:6]
    else:
        out_ref = rest[0]
    s_bufs, mx_bufs = rest[-11:-7], rest[-7:-3]
    m_ref, l_ref, acc_ref = rest[-3:]
    t = ATTN_TILE
    v_dims = 2 * C_HEAD_DIM if mode == "diff" else D_V_DIM
    v_rows = v_dims + ONES_ROWS
    n_pairs = n_tiles * (n_tiles + 1) // 2
    assert n_tiles >= 2 and n_pairs % 2 == 0
    feat = lax.broadcasted_iota(jnp.int32, (128, t), 0)

    def score(pair, buf, j):
        qi, kb = pair
        rows = pl.ds(pl.multiple_of(kb * t, t), t)
        if mode == "diff":
            qt = qt_ref[0, qi]
            mine = (feat < C_HEAD_DIM) if j == 0 else (feat >= C_HEAD_DIM)
            q, k = jnp.where(mine, qt, jnp.zeros_like(qt)), k_ref[0, rows, :]
        else:
            cols = slice(j * D_PAD_DIM, (j + 1) * D_PAD_DIM)
            q, k = qt_ref[0, qi, cols, :], k_ref[0, rows, cols]
        s = _dot(k, q)
        s_bufs[buf][j] = s
        mx_bufs[buf][j] = jnp.max(s, axis=0, keepdims=True)

    def finish_tile(qi):
        a1 = acc_ref[0] / l_ref[0]
        a2 = acc_ref[1] / l_ref[1]
        if mode == "diff":
            lam = (jnp.exp(jnp.sum(lq1_ref[...] * lk1_ref[...], axis=-1, keepdims=True))
                   - jnp.exp(jnp.sum(lq2_ref[...] * lk2_ref[...], axis=-1, keepdims=True))
                   + lam_init)
            d = a1 - lam * a2
            out = (d * lax.rsqrt(jnp.mean(d * d, axis=0, keepdims=True) + EPS) * g_ref[...]
                   * (1.0 - lam_init))
        else:
            out = jnp.concatenate([a1, a2], axis=0)
        out_ref[0, pl.ds(pl.multiple_of(qi * t, t), t), :] = out.T.astype(BF16)

    def consume(pair, buf, j, diagonal):
        qi, kb = pair
        first = kb == 0
        s = s_bufs[buf][j]
        if diagonal:
            kr = lax.broadcasted_iota(jnp.int32, (t, t), 0)
            qc = lax.broadcasted_iota(jnp.int32, (t, t), 1)
            s = jnp.where(kr <= qc, s, NEG)
            block_max = jnp.max(s, axis=0, keepdims=True)
        else:
            block_max = mx_bufs[buf][j]
        m = jnp.where(first, NEG, m_ref[j])
        l = jnp.where(first, 0.0, l_ref[j])
        m_new = jnp.maximum(m, block_max)
        alpha = jnp.exp2(m - m_new)
        p = jnp.exp2((s - m_new).astype(BF16))
        m_ref[j] = m_new
        if mode == "diff":
            vt = vt_ref[0, kb]
        else:
            vt = vt_ref[0, kb, j * v_rows:(j + 1) * v_rows, :]
        pv = _dot(vt, p)
        l_ref[j] = alpha * l + pv[v_dims:v_dims + 1]
        acc_ref[j] = alpha * acc_ref[j] + pv[:v_dims]
        if diagonal and j == 1:
            finish_tile(qi)

    def advance(pair):
        qi, kb = pair
        end = kb == qi
        nqi, nkb = jnp.where(end, qi + 1, qi), jnp.where(end, 0, kb + 1)
        done = nqi >= n_tiles
        return jnp.where(done, qi, nqi), jnp.where(done, kb, nkb)

    acc_ref[...] = jnp.zeros(acc_ref.shape, F32)
    pair0 = (jnp.int32(0), jnp.int32(0))
    pair1 = advance(pair0)
    for j in range(2):
        score(pair0, 0, j)
        score(pair1, 1, j)

    def half_trip(pa, pb, half):
        pc = advance(pb)
        pd = advance(pc)
        diag_a, diag_b = pa[0] == pa[1], pb[0] == pb[1]
        ra, rb, wc, wd = 2 * half, 2 * half + 1, 2 - 2 * half, 3 - 2 * half

        def run(da, db):
            score(pc, wc, 0)
            score(pc, wc, 1)
            consume(pa, ra, 0, da)
            score(pd, wd, 0)
            consume(pa, ra, 1, da)
            consume(pb, rb, 0, db)
            score(pd, wd, 1)
            consume(pb, rb, 1, db)

        @pl.when(diag_a | diag_b)
        def _():
            pl.when(diag_a)(lambda: run(True, False))
            pl.when(diag_b)(lambda: run(False, True))

        pl.when(jnp.logical_not(diag_a | diag_b))(lambda: run(False, False))
        return pc, pd

    def body(_, carry):
        pa, pb = carry[:2], carry[2:]
        for half in range(2):
            pa, pb = half_trip(pa, pb, half)
        return (*pa, *pb)

    assert n_pairs % 4 == 0
    lax.fori_loop(0, n_pairs // 4, body, (*pair0, *pair1))


def _causal_attention(qt, k, vt, mode, extras=(), lam_init=0.0):
    b, s, _ = k.shape
    t = ATTN_TILE
    assert qt.shape[3] == t and vt.shape[3] == t
    qw = 128 if mode == "diff" else 2 * D_PAD_DIM
    groups = k.shape[2] // qw
    in_specs = [
        pl.BlockSpec((1, s // t, qw, t), lambda bi, h: (bi, 0, h, 0)),
        pl.BlockSpec((1, s, qw), lambda bi, h: (bi, 0, h)),
        pl.BlockSpec((1, s // t, vt.shape[2] // groups, t), lambda bi, h: (bi, 0, h, 0)),
    ] + [_const_spec(e.shape) for e in extras]
    stats = [pltpu.VMEM((2, 1, t), F32)] * 2
    return pl.pallas_call(
        functools.partial(_causal_kernel, mode=mode, lam_init=lam_init, n_tiles=s // t),
        grid=(b, groups),
        in_specs=in_specs,
        out_specs=pl.BlockSpec((1, s, 128), lambda bi, h: (bi, 0, h)),
        out_shape=jax.ShapeDtypeStruct((b, s, groups * 128), BF16),
        scratch_shapes=[
            *[pltpu.VMEM((2, t, t), F32)] * 4,
            *stats, *stats,
            *stats,
            pltpu.VMEM((2, 128 if mode == "diff" else D_V_DIM, t), F32),
        ],
        compiler_params=_cparams(("parallel", "parallel")),
        name=f"causal_attn_{mode}",
    )(qt, k, vt, *extras)


def _cd_out_kernel(x_ref, yc_ref, yd_ref, wc_ref, wd_ref, out_ref):
    out_ref[...] = x_ref[...] + _dot(yc_ref[...], wc_ref[...]) + _dot(yd_ref[...], wd_ref[...])


def _cd_out(x2, yc, yd, w_out):
    t = x2.shape[0]
    row = lambda w: pl.BlockSpec((WIDE_ROW_TILE, w), lambda i: (i, 0))
    dvw = D_HEADS * D_V_DIM
    return pl.pallas_call(
        _cd_out_kernel,
        grid=(t // WIDE_ROW_TILE,),
        in_specs=[row(D_MODEL), row(C_WIDTH), row(dvw),
                  _const_spec((C_WIDTH, D_MODEL)), _const_spec((dvw, D_MODEL))],
        out_specs=row(D_MODEL),
        out_shape=jax.ShapeDtypeStruct((t, D_MODEL), F32),
        compiler_params=_cparams(("parallel",)),
        name="cd_out_proj",
    )(x2, yc, yd, w_out[:C_WIDTH].astype(BF16), w_out[C_WIDTH:].astype(BF16))


def kernel(x, mem, positions, norm_mix_g, norm_cross_g, norm_mem_g, cross_wq, cross_wkv, cross_wo,
           norm_mlp_g, mlp_w1, mlp_w2, ab_w_in, ab_w_out, ab_conv_w, ab_conv_b, ab_ln_g, ab_ln_b,
           cd_w_in, cd_w_out, diff_lq1, diff_lk1, diff_lq2, diff_lk2, diff_subln_g, mla_q_norm_g,
           mla_kv_norm_g, mla_w_uq, mla_w_uk, mla_w_uv, final_norm_g):
    b, s, d = x.shape
    t = b * s
    mem_len = mem.shape[1]
    depth = norm_mix_g.shape[0]
    xw = X_HEADS * X_HEAD_DIM
    x2 = x.reshape(t, d)
    for i in range(depth):
        j = i // 2
        if i % 2 == 0:
            *views, glu = _ab_in(x2.reshape(b, s, d), norm_mix_g[i], ab_w_in[j])
            outs, lses = zip(*[_dilated_pattern(view, dil)
                               for view, (_, dil) in zip(views, A_PATTERNS)])
            yb = _conv_module(glu, ab_conv_w[j], ab_conv_b[j], ab_ln_g[j], ab_ln_b[j])
            x2 = _ab_out(x2.reshape(b, s, d), outs, lses, yb, ab_w_out[j]).reshape(t, d)
        else:
            qc, kc, vc, qd, kd, vd = _cd_in(
                x2.reshape(b, s, d), positions, norm_mix_g[i], cd_w_in[j], mla_q_norm_g[j],
                mla_kv_norm_g[j], mla_w_uq[j], mla_w_uk[j], mla_w_uv[j])
            lam_init = 0.8 - 0.6 * math.exp(-0.3 * i)
            vec = lambda a: a.reshape(1, -1)
            yc = _causal_attention(
                qc, kc, vc, "diff",
                extras=(vec(diff_lq1[j]), vec(diff_lk1[j]), vec(diff_lq2[j]), vec(diff_lk2[j]),
                        diff_subln_g[j].reshape(-1, 1)),
                lam_init=lam_init)
            yd = _causal_attention(qd, kd, vd, "mla")
            x2 = _cd_out(x2, yc.reshape(t, C_WIDTH), yd.reshape(t, D_HEADS * D_V_DIM), cd_w_out[j])
        km, vm = _mem_kv(mem.reshape(b * mem_len, d), norm_mem_g[i], cross_wkv[i])
        x2 = _cross_attention(x2.reshape(b, s, d), norm_cross_g[i], cross_wq[i],
                              km.reshape(b, mem_len, xw), vm.reshape(b, mem_len, xw),
                              cross_wo[i]).reshape(t, d)
        x2 = _mlp(x2, norm_mlp_g[i], mlp_w1[i], mlp_w2[i], final_norm_g, i == depth - 1)
    return x2.reshape(b, s, d)
```
